```python
import jax, jax.numpy as jnp
from jax import lax
import numpy as np

D_MODEL = 2048
BATCH = 4
SEQ = 4096
DEPTH = 2

CTX_LEN = 256
GRID_W = 64
D_MIX = D_MODEL
N_GROUPS = 4
GROUP_W = D_MIX // N_GROUPS
CONV_W = GROUP_W
CONV_K = 3
SWA_HEAD_DIM = 64
SWA_HEADS = GROUP_W // SWA_HEAD_DIM
SWA_KV_HEADS = 2
SWA_WINDOW = 128
SWA_BLOCK = 128
FNET_HEAD_DIM = 64
FNET_HEADS = GROUP_W // FNET_HEAD_DIM
MLA_HEADS = 8
MLA_NOPE = 64
MLA_ROPE = 32
MLA_V = GROUP_W // MLA_HEADS
MLA_Q_LORA = 512
MLA_KV_LORA = 256
MLA_BLOCK = 128
N_EXPERTS = 16
EXPERT_FF = 2048
EC_CAPACITY = 2
ROPE_THETA = 10000.0
EPS = 1e-6

IN_SIZES = (CONV_W, CONV_W, CONV_W,
            SWA_HEADS * SWA_HEAD_DIM, SWA_KV_HEADS * SWA_HEAD_DIM, SWA_KV_HEADS * SWA_HEAD_DIM,
            GROUP_W,
            MLA_Q_LORA, MLA_KV_LORA, MLA_ROPE)
IN_COLS = 3 * CONV_W + (SWA_HEADS + 2 * SWA_KV_HEADS) * SWA_HEAD_DIM + GROUP_W + MLA_Q_LORA + MLA_KV_LORA + MLA_ROPE

kernel_name = "hybrid_parallel_groups_ec_moe_diffusion"


def rmsnorm(x, g):
    xf = x.astype(jnp.float32)
    y = xf * lax.rsqrt(jnp.mean(xf * xf, axis=-1, keepdims=True) + EPS)
    return (y * g.astype(jnp.float32)).astype(x.dtype)


def group_rmsnorm(y, g):
    shp = y.shape
    out = rmsnorm(y.reshape(shp[:-1] + (N_GROUPS, GROUP_W)), g.reshape(N_GROUPS, GROUP_W))
    return out.reshape(shp)


def modulate(h, shift, scale):
    return h * (1.0 + scale) + shift


def axial_rope_tables(rows_n, rot_dim):
    rows = jnp.broadcast_to(jnp.arange(rows_n, dtype=jnp.float32)[:, None], (rows_n, GRID_W)).reshape(-1)
    cols = jnp.broadcast_to(jnp.arange(GRID_W, dtype=jnp.float32)[None, :], (rows_n, GRID_W)).reshape(-1)
    n_freq = rot_dim // 4
    inv_freq = ROPE_THETA ** (-jnp.arange(n_freq, dtype=jnp.float32) / n_freq)
    ang = jnp.concatenate([rows[:, None] * inv_freq, cols[:, None] * inv_freq], axis=-1)
    return jnp.cos(ang), jnp.sin(ang)


def apply_rope(x, cos, sin):
    half = x.shape[-1] // 2
    cs = cos[None, :, None, :].astype(x.dtype)
    sn = sin[None, :, None, :].astype(x.dtype)
    x1, x2 = x[..., :half], x[..., half:]
    return jnp.concatenate([x1 * cs - x2 * sn, x2 * cs + x1 * sn], axis=-1)


def mixer_inputs(h, w_in, q_norm_g, w_uq, kv_norm_g, w_ukv, rope):
    B, N, _ = h.shape
    p = h @ w_in
    parts, start = [], 0
    for width in IN_SIZES:
        parts.append(p[..., start:start + width])
        start += width
    cb, cc, ch, sq, sk, sv, fu, mcq, mckv, mkpe = parts
    sq = sq.reshape(B, N, SWA_HEADS, SWA_HEAD_DIM)
    sk = sk.reshape(B, N, SWA_KV_HEADS, SWA_HEAD_DIM)
    sv = sv.reshape(B, N, SWA_KV_HEADS, SWA_HEAD_DIM)
    q = (rmsnorm(mcq, q_norm_g) @ w_uq).reshape(B, N, MLA_HEADS, MLA_NOPE + MLA_ROPE)
    kv = (rmsnorm(mckv, kv_norm_g) @ w_ukv).reshape(B, N, MLA_HEADS, MLA_NOPE + MLA_V)
    q_nope, q_pe = q[..., :MLA_NOPE], q[..., MLA_NOPE:]
    k_nope, mv = kv[..., :MLA_NOPE], kv[..., MLA_NOPE:]
    k_pe = mkpe[:, :, None, :]
    if rope is not None:
        cos_s, sin_s, cos_m, sin_m = rope
        sq = apply_rope(sq, cos_s, sin_s)
        sk = apply_rope(sk, cos_s, sin_s)
        q_pe = apply_rope(q_pe, cos_m, sin_m)
        k_pe = apply_rope(k_pe, cos_m, sin_m)
    mq = jnp.concatenate([q_nope, q_pe], axis=-1)
    mk = jnp.concatenate([k_nope, jnp.broadcast_to(k_pe, (B, N, MLA_HEADS, MLA_ROPE))], axis=-1)
    return cb, cc, ch, sq, sk, sv, fu, mq, mk, mv


def short_conv_mixer(gate_b, gate_c, h, w):
    u = gate_c * h
    up = jnp.pad(u, ((0, 0), (1, 1), (0, 0)))
    y = up[:, :-2] * w[0] + up[:, 1:-1] * w[1] + up[:, 2:] * w[2]
    return gate_b * y


def fourier_mixer(u):
    B, N, _ = u.shape
    uh = u.reshape(B, N, FNET_HEADS, FNET_HEAD_DIM).astype(jnp.float32)
    y = jnp.fft.fft2(uh, axes=(1, 3), norm="ortho").real
    return y.reshape(B, N, GROUP_W).astype(u.dtype)


def swa_latent(q, k, v, kc, vc, sink):
    B, N, Hq, d = q.shape
    Hkv = k.shape[2]
    G = Hq // Hkv
    T = SWA_BLOCK
    nb = N // T
    scale = d ** -0.5
    qb = q.reshape(B, nb, T, Hkv, G, d)

    def band(t):
        tp = jnp.pad(t, ((0, 0), (T, T), (0, 0), (0, 0))).reshape(B, nb + 2, T, Hkv, d)
        return jnp.concatenate([tp[:, :-2], tp[:, 1:-1], tp[:, 2:]], axis=2)

    kb, vb = band(k), band(v)
    s_loc = jnp.einsum('bnqhgd,bnkhd->bnhgqk', qb, kb, preferred_element_type=jnp.float32) * scale
    qpos = jnp.arange(nb)[:, None] * T + jnp.arange(T)[None, :]
    kpos = jnp.arange(nb)[:, None] * T - T + jnp.arange(3 * T)[None, :]
    rel = kpos[:, None, :] - qpos[:, :, None]
    valid = (jnp.abs(rel) <= SWA_WINDOW) & (kpos[:, None, :] >= 0) & (kpos[:, None, :] < N)
    s_loc = jnp.where(valid[None, :, None, None], s_loc, -1e30)
    s_ctx = jnp.einsum('bnqhgd,blhd->bnhgql', qb, kc, preferred_element_type=jnp.float32) * scale
    s_sink = jnp.broadcast_to(sink.astype(jnp.float32).reshape(1, 1, Hkv, G, 1, 1), s_loc.shape[:-1] + (1,))
    p = jax.nn.softmax(jnp.concatenate([s_loc, s_ctx, s_sink], axis=-1), axis=-1)
    p_loc = p[..., :3 * T].astype(v.dtype)
    p_ctx = p[..., 3 * T:3 * T + kc.shape[1]].astype(v.dtype)
    o = (jnp.einsum('bnhgqk,bnkhd->bnqhgd', p_loc, vb)
         + jnp.einsum('bnhgql,blhd->bnqhgd', p_ctx, vc))
    return o.reshape(B, N, Hq * d)


def swa_context(q, k, v, sink):
    B, L, Hq, d = q.shape
    Hkv = k.shape[2]
    G = Hq // Hkv
    qg = q.reshape(B, L, Hkv, G, d)
    s = jnp.einsum('blhgd,bmhd->bhglm', qg, k, preferred_element_type=jnp.float32) * (d ** -0.5)
    s_sink = jnp.broadcast_to(sink.astype(jnp.float32).reshape(1, Hkv, G, 1, 1), (B, Hkv, G, L, 1))
    p = jax.nn.softmax(jnp.concatenate([s, s_sink], axis=-1), axis=-1)[..., :L].astype(v.dtype)
    o = jnp.einsum('bhglm,bmhd->blhgd', p, v)
    return o.reshape(B, L, Hq * d)


def mla_latent(q, k, v, kc, vc):
    B, N, H, dq = q.shape
    nb = N // MLA_BLOCK
    scale = dq ** -0.5
    qb = q.reshape(B, nb, MLA_BLOCK, H, dq).transpose(1, 0, 2, 3, 4)

    def block(qi):
        s_lat = jnp.einsum('bqhd,bkhd->bhqk', qi, k, preferred_element_type=jnp.float32) * scale
        s_ctx = jnp.einsum('bqhd,blhd->bhql', qi, kc, preferred_element_type=jnp.float32) * scale
        p = jax.nn.softmax(jnp.concatenate([s_lat, s_ctx], axis=-1), axis=-1).astype(v.dtype)
        return (jnp.einsum('bhqk,bkhd->bqhd', p[..., :N], v)
                + jnp.einsum('bhql,blhd->bqhd', p[..., N:], vc))

    o = lax.map(block, qb)
    return o.transpose(1, 0, 2, 3, 4).reshape(B, N, H * v.shape[-1])


def mla_context(q, k, v):
    B, L, H, dq = q.shape
    s = jnp.einsum('blhd,bmhd->bhlm', q, k, preferred_element_type=jnp.float32) * (dq ** -0.5)
    p = jax.nn.softmax(s, axis=-1).astype(v.dtype)
    return jnp.einsum('bhlm,bmhd->blhd', p, v).reshape(B, L, H * v.shape[-1])


def mix_latent(px, pc, conv_w, sink):
    cb, cc, ch, sq, sk, sv, fu, mq, mk, mv = px
    skc, svc, mkc, mvc = pc[4], pc[5], pc[8], pc[9]
    return jnp.concatenate([
        short_conv_mixer(cb, cc, ch, conv_w),
        swa_latent(sq, sk, sv, skc, svc, sink),
        fourier_mixer(fu),
        mla_latent(mq, mk, mv, mkc, mvc),
    ], axis=-1)


def mix_context(pc, conv_w, sink):
    cb, cc, ch, sq, sk, sv, fu, mq, mk, mv = pc
    return jnp.concatenate([
        short_conv_mixer(cb, cc, ch, conv_w),
        swa_context(sq, sk, sv, sink),
        fourier_mixer(fu),
        mla_context(mq, mk, mv),
    ], axis=-1)


def expert_choice_ffn(h, w_router, w_gate, w_up, w_down):
    B, N, D = h.shape
    cap = EC_CAPACITY * N // N_EXPERTS
    aff = jax.nn.softmax(jnp.einsum('bnd,de->bne', h, w_router, preferred_element_type=jnp.float32), axis=-1)
    g, idx = lax.top_k(jnp.swapaxes(aff, 1, 2), cap)
    xs = jax.vmap(lambda hb, ib: hb[ib])(h, idx)
    a = jnp.einsum('becd,edf->becf', xs, w_gate)
    u = jnp.einsum('becd,edf->becf', xs, w_up)
    y = jnp.einsum('becf,efd->becd', jax.nn.silu(a) * u, w_down)
    y = y * g[..., None].astype(y.dtype)
    return jax.vmap(lambda ib, yb: jnp.zeros((N, D), yb.dtype).at[ib.reshape(-1)].add(yb.reshape(-1, D)))(idx, y)


def setup_inputs(seed: int = 0) -> dict:
    key = jax.random.key(seed)
    ks = jax.random.split(key, 24)
    f32 = jnp.float32

    def nrm(k, shape, fan_in):
        return jax.random.normal(k, shape, f32) * (fan_in ** -0.5)

    def gain(k, shape):
        return 1.0 + 0.02 * jax.random.normal(k, shape, f32)

    return {
        "x": jax.random.normal(ks[0], (BATCH, SEQ, D_MODEL), f32),
        "c": jax.random.normal(ks[1], (BATCH, D_MODEL), f32),
        "ctx": jax.random.normal(ks[2], (BATCH, CTX_LEN, D_MODEL), f32),
        "c_ctx": jax.random.normal(ks[3], (D_MODEL,), f32),
        "ada_w": nrm(ks[4], (DEPTH, D_MODEL, 6 * D_MODEL), D_MODEL),
        "ada_b": 0.01 * jax.random.normal(ks[5], (DEPTH, 6 * D_MODEL), f32),
        "norm1_g": gain(ks[6], (DEPTH, D_MODEL)),
        "norm2_g": gain(ks[7], (DEPTH, D_MODEL)),
        "w_in": nrm(ks[8], (DEPTH, D_MODEL, IN_COLS), D_MODEL),
        "conv_w": nrm(ks[9], (DEPTH, CONV_K, CONV_W), CONV_K),
        "swa_sink": 0.5 * jax.random.normal(ks[10], (DEPTH, SWA_HEADS), f32),
        "mla_q_norm_g": gain(ks[11], (DEPTH, MLA_Q_LORA)),
        "mla_w_uq": nrm(ks[12], (DEPTH, MLA_Q_LORA, MLA_HEADS * (MLA_NOPE + MLA_ROPE)), MLA_Q_LORA),
        "mla_kv_norm_g": gain(ks[13], (DEPTH, MLA_KV_LORA)),
        "mla_w_ukv": nrm(ks[14], (DEPTH, MLA_KV_LORA, MLA_HEADS * (MLA_NOPE + MLA_V)), MLA_KV_LORA),
        "out_norm_g": gain(ks[15], (DEPTH, D_MIX)),
        "w_out": nrm(ks[16], (DEPTH, D_MIX, D_MODEL), D_MIX),
        "w_router": nrm(ks[17], (DEPTH, D_MODEL, N_EXPERTS), D_MODEL),
        "w_gate": nrm(ks[18], (DEPTH, N_EXPERTS, D_MODEL, EXPERT_FF), D_MODEL),
        "w_up": nrm(ks[19], (DEPTH, N_EXPERTS, D_MODEL, EXPERT_FF), D_MODEL),
        "w_down": nrm(ks[20], (DEPTH, N_EXPERTS, EXPERT_FF, D_MODEL), EXPERT_FF),
        "final_norm_g": gain(ks[21], (D_MODEL,)),
    }


def reference(x, c, ctx, c_ctx, ada_w, ada_b, norm1_g, norm2_g, w_in, conv_w, swa_sink,
              mla_q_norm_g, mla_w_uq, mla_kv_norm_g, mla_w_ukv, out_norm_g, w_out,
              w_router, w_gate, w_up, w_down, final_norm_g):
    n_tok = x.shape[1]
    ROWS = n_tok // GRID_W
    cos_s, sin_s = axial_rope_tables(ROWS, SWA_HEAD_DIM)
    cos_m, sin_m = axial_rope_tables(ROWS, MLA_ROPE)
    rope = (cos_s, sin_s, cos_m, sin_m)
    for layer in range(DEPTH):
        last = layer == DEPTH - 1
        mx = jnp.split(jax.nn.silu(c) @ ada_w[layer] + ada_b[layer], 6, axis=-1)
        sh1, sc1, g1, sh2, sc2, g2 = [m[:, None, :] for m in mx]
        csh1, csc1, cg1, csh2, csc2, cg2 = jnp.split(
            jax.nn.silu(c_ctx) @ ada_w[layer] + ada_b[layer], 6, axis=-1)

        hx = modulate(rmsnorm(x, norm1_g[layer]), sh1, sc1)
        hc = modulate(rmsnorm(ctx, norm1_g[layer]), csh1, csc1)
        px = mixer_inputs(hx, w_in[layer], mla_q_norm_g[layer], mla_w_uq[layer],
                          mla_kv_norm_g[layer], mla_w_ukv[layer], rope)
        pc = mixer_inputs(hc, w_in[layer], mla_q_norm_g[layer], mla_w_uq[layer],
                          mla_kv_norm_g[layer], mla_w_ukv[layer], None)
        yx = mix_latent(px, pc, conv_w[layer], swa_sink[layer])
        x = x + g1 * (group_rmsnorm(yx, out_norm_g[layer]) @ w_out[layer])
        if not last:
            yc = mix_context(pc, conv_w[layer], swa_sink[layer])
            ctx = ctx + cg1 * (group_rmsnorm(yc, out_norm_g[layer]) @ w_out[layer])

        fx = modulate(rmsnorm(x, norm2_g[layer]), sh2, sc2)
        x = x + g2 * expert_choice_ffn(fx, w_router[layer], w_gate[layer], w_up[layer], w_down[layer])
        if not last:
            fc = modulate(rmsnorm(ctx, norm2_g[layer]), csh2, csc2)
            ctx = ctx + cg2 * expert_choice_ffn(fc, w_router[layer], w_gate[layer], w_up[layer], w_down[layer])
    return rmsnorm(x, final_norm_g)
```

```python
import functools
import math

import jax
import jax.numpy as jnp
from jax import lax
from jax.experimental import pallas as pl
from jax.experimental.pallas import tpu as pltpu

F32 = jnp.float32
BF16 = jnp.bfloat16
HIGHEST = lax.Precision.HIGHEST

GRID_W = 64
GROUP_W = 512
N_GROUPS = 4
SWA_HEAD_DIM = 64
SWA_HEADS = 8
SWA_KV_HEADS = 2
SWA_BLOCK = 128
MLA_HEADS = 8
MLA_NOPE = 64
MLA_ROPE = 32
MLA_V = 64
MLA_Q_LORA = 512
MLA_KV_LORA = 256
N_EXPERTS = 16
EC_CAPACITY = 2
ROPE_THETA = 10000.0
EPS = 1e-6
IN_COLS = 3616
IN_COLS_PAD = 3712
LANES = 128
NEG = -1e30

VMEM_LIMIT = 60 * 1024 * 1024


def _cparams(sem):
    return pltpu.CompilerParams(dimension_semantics=sem, vmem_limit_bytes=VMEM_LIMIT)


def _dot(a, b):
    return jnp.dot(a, b, preferred_element_type=F32)


def _dot_t(a, b):
    return lax.dot_general(a, b, (((1,), (1,)), ((), ())), preferred_element_type=F32)


def _rms(x, g):
    return x * lax.rsqrt(jnp.mean(x * x, axis=-1, keepdims=True) + EPS) * g


def _adaln_kernel(c_ref, w_ref, b_ref, o_ref):
    c = c_ref[...]
    s = c / (1.0 + jnp.exp(-c))
    o_ref[0] = lax.dot_general(s, w_ref[0], (((1,), (0,)), ((), ())),
                               precision=HIGHEST, preferred_element_type=F32) + b_ref[0]


def _adaln(c_rows, ada_w, ada_b):
    depth, d, d6 = ada_w.shape
    tn = 1024 if d6 % 1024 == 0 else d6
    rows = c_rows.shape[0]
    return pl.pallas_call(
        _adaln_kernel,
        grid=(depth, d6 // tn),
        in_specs=[
            pl.BlockSpec((rows, d), lambda l, j: (0, 0)),
            pl.BlockSpec((1, d, tn), lambda l, j: (l, 0, j)),
            pl.BlockSpec((1, 1, tn), lambda l, j: (l, 0, j)),
        ],
        out_specs=pl.BlockSpec((1, rows, tn), lambda l, j: (l, 0, j)),
        out_shape=jax.ShapeDtypeStruct((depth, rows, d6), F32),
        compiler_params=_cparams(("arbitrary", "arbitrary")),
        name="adaln",
    )(c_rows, ada_w, ada_b.reshape(depth, 1, d6))


def _swap_halves(x, width):
    lane = lax.broadcasted_iota(jnp.int32, x.shape, 1)
    first = (lane % (2 * width)) < width
    return jnp.where(first, pltpu.roll(x, LANES - width, 1), pltpu.roll(x, width, 1))


def _mixin_kernel(x_ref, sh_ref, sc_ref, g_ref, w_ref, qg_ref, kvg_ref, wuq_ref, wuk_ref, wuv_ref,
                  css_ref, sns_ref, csm_ref, snm_ref,
                  cb_ref, u_ref, sq_ref, skp_ref, svp_ref, fu_ref, mq_ref, mk_ref, mvp_ref):
    x = x_ref[0]
    h = _rms(x, g_ref[...]) * (1.0 + sc_ref[0]) + sh_ref[0]
    hb = h.astype(BF16)

    pc = _dot(hb, w_ref[:, 0:3 * GROUP_W])
    cb_ref[0] = pc[:, 0:GROUP_W].astype(BF16)
    u_ref[0] = (pc[:, GROUP_W:2 * GROUP_W] * pc[:, 2 * GROUP_W:3 * GROUP_W]).astype(BF16)

    c0 = 3 * GROUP_W
    ps = _dot(hb, w_ref[:, c0:c0 + 768])
    css = css_ref[...]
    sns = sns_ref[...]
    qscale = SWA_HEAD_DIM ** -0.5
    for c in range(4):
        t = ps[:, c * LANES:(c + 1) * LANES]
        t = (t * css + _swap_halves(t, SWA_HEAD_DIM // 2) * sns) * qscale
        sq_ref[0, :, c * LANES:(c + 1) * LANES] = t.astype(BF16)
    sk = ps[:, 512:640]
    sk = sk * css + _swap_halves(sk, SWA_HEAD_DIM // 2) * sns
    sv = ps[:, 640:768]
    lane = lax.broadcasted_iota(jnp.int32, sk.shape, 1)
    lo = lane < SWA_HEAD_DIM
    for src, dst in ((sk, skp_ref), (sv, svp_ref)):
        sw = pltpu.roll(src, SWA_HEAD_DIM, 1)
        zero = jnp.zeros_like(src)
        dst[0, :, 0:128] = jnp.where(lo, src, zero).astype(BF16)
        dst[0, :, 128:256] = jnp.where(lo, zero, sw).astype(BF16)
        dst[0, :, 256:384] = jnp.where(lo, sw, zero).astype(BF16)
        dst[0, :, 384:512] = jnp.where(lo, zero, src).astype(BF16)

    c0 += 768
    fu_ref[0] = _dot(hb, w_ref[:, c0:c0 + GROUP_W]).astype(BF16)

    c0 += GROUP_W
    pm = _dot(hb, w_ref[:, c0:c0 + 896])
    qn = _rms(pm[:, 0:MLA_Q_LORA], qg_ref[...]).astype(BF16)
    kvn = _rms(pm[:, MLA_Q_LORA:MLA_Q_LORA + MLA_KV_LORA], kvg_ref[...]).astype(BF16)
    csm = csm_ref[...]
    snm = snm_ref[...]

    def rope_m(t):
        lane_m = lax.broadcasted_iota(jnp.int32, t.shape, 1)
        sw = jnp.where(lane_m < MLA_NOPE + MLA_ROPE // 2,
                       pltpu.roll(t, LANES - MLA_ROPE // 2, 1), pltpu.roll(t, MLA_ROPE // 2, 1))
        return t * csm + sw * snm

    q = _dot(qn, wuq_ref[...])
    mscale = (MLA_NOPE + MLA_ROPE) ** -0.5
    kpe = rope_m(pltpu.roll(pm[:, 768:896], MLA_NOPE, 1))
    kn = _dot(kvn, wuk_ref[...])
    for hd in range(MLA_HEADS):
        sl = slice(hd * LANES, (hd + 1) * LANES)
        mq_ref[0, :, sl] = (rope_m(q[:, sl]) * mscale).astype(BF16)
        mk_ref[0, :, sl] = (kn[:, sl] + kpe).astype(BF16)
    mvp_ref[0] = _dot(kvn, wuv_ref[...]).astype(BF16)


def _mixer_inputs(x, shift, scale, norm_g, w_in_p, qg, kvg, wuq, wuk, wuv, rope_tabs):
    b, n, d = x.shape
    tm = min(256, n)
    css, sns, csm, snm = rope_tabs
    row = lambda bi, i: (bi, i, 0)
    const2 = lambda bi, i: (0, 0)
    tab = lambda bi, i: (i, 0)
    outs = [jax.ShapeDtypeStruct((b, n, w), BF16) for w in (512, 512, 512, 512, 512, 512, 1024, 1024, 1024)]
    return pl.pallas_call(
        _mixin_kernel,
        grid=(b, n // tm),
        in_specs=[
            pl.BlockSpec((1, tm, d), row),
            pl.BlockSpec((1, 1, d), lambda bi, i: (bi, 0, 0)),
            pl.BlockSpec((1, 1, d), lambda bi, i: (bi, 0, 0)),
            pl.BlockSpec((1, d), const2),
            pl.BlockSpec((d, IN_COLS_PAD), const2),
            pl.BlockSpec((1, MLA_Q_LORA), const2),
            pl.BlockSpec((1, MLA_KV_LORA), const2),
            pl.BlockSpec((MLA_Q_LORA, 1024), const2),
            pl.BlockSpec((MLA_KV_LORA, 1024), const2),
            pl.BlockSpec((MLA_KV_LORA, 1024), const2),
            pl.BlockSpec((tm, LANES), tab),
            pl.BlockSpec((tm, LANES), tab),
            pl.BlockSpec((tm, LANES), tab),
            pl.BlockSpec((tm, LANES), tab),
        ],
        out_specs=[pl.BlockSpec((1, tm, o.shape[-1]), row) for o in outs],
        out_shape=outs,
        compiler_params=_cparams(("arbitrary", "arbitrary")),
        name="mixer_inputs",
    )(x, shift, scale, norm_g, w_in_p, qg, kvg, wuq, wuk, wuv, css, sns, csm, snm)


def _swa_kernel(sink_ref, q_ref, kc_ref, vc_ref, *rest, has_band, n_blocks):
    if has_band:
        kp_ref, vp_ref, o_ref = rest
    else:
        (o_ref,) = rest
    i = pl.program_id(1)
    t = SWA_BLOCK
    if has_band:
        ql = lax.broadcasted_iota(jnp.int32, (t, t), 0)
        kl = lax.broadcasted_iota(jnp.int32, (t, t), 1)
        mask_prev = (kl >= ql) & (i > 0)
        mask_next = (kl <= ql) & (i < n_blocks - 1)
        r_prev = pl.multiple_of(jnp.maximum(i - 1, 0) * t, t)
        r_cur = pl.multiple_of(i * t, t)
        r_next = pl.multiple_of(jnp.minimum(i + 1, n_blocks - 1) * t, t)
    for c in range(4):
        qc = q_ref[0, :, c * LANES:(c + 1) * LANES]
        acc = jnp.zeros((t, LANES), F32)
        for hh in range(2):
            head = 2 * c + hh
            slot = 2 * (c // 2) + hh
            sl = slice(slot * LANES, (slot + 1) * LANES)
            s_ctx = _dot_t(qc, kc_ref[0, :, sl])
            sink = sink_ref[head]
            m = jnp.maximum(jnp.max(s_ctx, axis=-1, keepdims=True), sink)
            if has_band:
                s_p = jnp.where(mask_prev, _dot_t(qc, kp_ref[0, pl.ds(r_prev, t), sl]), NEG)
                s_c = _dot_t(qc, kp_ref[0, pl.ds(r_cur, t), sl])
                s_n = jnp.where(mask_next, _dot_t(qc, kp_ref[0, pl.ds(r_next, t), sl]), NEG)
                m = jnp.maximum(m, jnp.max(jnp.maximum(jnp.maximum(s_p, s_c), s_n), axis=-1, keepdims=True))
            p_ctx = jnp.exp(s_ctx - m)
            den = jnp.sum(p_ctx, axis=-1, keepdims=True) + jnp.exp(sink - m)
            o = _dot(p_ctx.astype(BF16), vc_ref[0, :, sl])
            if has_band:
                p_p = jnp.exp(s_p - m)
                p_c = jnp.exp(s_c - m)
                p_n = jnp.exp(s_n - m)
                den = den + jnp.sum(p_p + p_c + p_n, axis=-1, keepdims=True)
                o = o + _dot(p_p.astype(BF16), vp_ref[0, pl.ds(r_prev, t), sl])
                o = o + _dot(p_c.astype(BF16), vp_ref[0, pl.ds(r_cur, t), sl])
                o = o + _dot(p_n.astype(BF16), vp_ref[0, pl.ds(r_next, t), sl])
            acc = acc + o / den
        o_ref[0, :, c * LANES:(c + 1) * LANES] = acc.astype(BF16)


def _swa(sink, q, kc, vc, kp=None, vp=None):
    b, n, _ = q.shape
    l = kc.shape[1]
    t = SWA_BLOCK
    has_band = kp is not None
    blk = lambda bi, i: (bi, i, 0)
    whole = lambda bi, i: (bi, 0, 0)
    in_specs = [
        pl.BlockSpec(memory_space=pltpu.SMEM),
        pl.BlockSpec((1, t, 512), blk),
        pl.BlockSpec((1, l, 512), whole),
        pl.BlockSpec((1, l, 512), whole),
    ]
    args = [sink, q, kc, vc]
    if has_band:
        in_specs += [pl.BlockSpec((1, n, 512), whole), pl.BlockSpec((1, n, 512), whole)]
        args += [kp, vp]
    return pl.pallas_call(
        functools.partial(_swa_kernel, has_band=has_band, n_blocks=n // t),
        grid=(b, n // t),
        in_specs=in_specs,
        out_specs=pl.BlockSpec((1, t, 512), blk),
        out_shape=jax.ShapeDtypeStruct((b, n, 512), BF16),
        compiler_params=_cparams(("arbitrary", "arbitrary")),
        name="swa_latent" if has_band else "swa_context",
    )(*args)


def _mla_kernel(q_ref, kc_ref, vc_ref, *rest, has_lat):
    if has_lat:
        k_ref, v_ref, o_ref = rest
    else:
        (o_ref,) = rest
    q = q_ref[0]
    s_ctx = _dot_t(q, kc_ref[0])
    m = jnp.max(s_ctx, axis=-1, keepdims=True)
    if has_lat:
        s_lat = _dot_t(q, k_ref[0])
        m = jnp.maximum(m, jnp.max(s_lat, axis=-1, keepdims=True))
    p_ctx = jnp.exp(s_ctx - m)
    den = jnp.sum(p_ctx, axis=-1, keepdims=True)
    o = _dot(p_ctx.astype(BF16), vc_ref[0])
    if has_lat:
        p_lat = jnp.exp(s_lat - m)
        den = den + jnp.sum(p_lat, axis=-1, keepdims=True)
        o = o + _dot(p_lat.astype(BF16), v_ref[0])
    o_ref[0] = (o / den).astype(BF16)


def _mla(q, kc, vc, k=None, v=None):
    b, n, _ = q.shape
    l = kc.shape[1]
    tq = min(256, n)
    has_lat = k is not None
    in_specs = [
        pl.BlockSpec((1, tq, LANES), lambda bi, h, i: (bi, i, h)),
        pl.BlockSpec((1, l, LANES), lambda bi, h, i: (bi, 0, h)),
        pl.BlockSpec((1, l, LANES), lambda bi, h, i: (bi, 0, h)),
    ]
    args = [q, kc, vc]
    if has_lat:
        in_specs += [pl.BlockSpec((1, n, LANES), lambda bi, h, i: (bi, 0, h)),
                     pl.BlockSpec((1, n, LANES), lambda bi, h, i: (bi, 0, h))]
        args += [k, v]
    return pl.pallas_call(
        functools.partial(_mla_kernel, has_lat=has_lat),
        grid=(b, MLA_HEADS, n // tq),
        in_specs=in_specs,
        out_specs=pl.BlockSpec((1, tq, LANES), lambda bi, h, i: (bi, i, h)),
        out_shape=jax.ShapeDtypeStruct((b, n, MLA_HEADS * LANES), BF16),
        compiler_params=_cparams(("arbitrary", "arbitrary", "arbitrary")),
        name="mla_latent" if has_lat else "mla_context",
    )(*args)


def _mixout_kernel(x_ref, g1_ref, yc_ref, ys_ref, yf_ref, ym_ref, og_ref, wo_ref,
                   n2_ref, sh2_ref, sc2_ref, wr_ref, xo_ref, fx_ref, aff_ref):
    ym = ym_ref[0].astype(F32)
    mla = jnp.concatenate(
        [ym[:, (2 * c) * LANES:(2 * c + 1) * LANES] + ym[:, (2 * c + 1) * LANES:(2 * c + 2) * LANES]
         for c in range(4)], axis=-1)
    groups = (yc_ref[0].astype(F32), ys_ref[0].astype(F32), yf_ref[0].astype(F32), mla)
    acc = None
    for gi, y in enumerate(groups):
        sl = slice(gi * GROUP_W, (gi + 1) * GROUP_W)
        yn = _rms(y, og_ref[:, sl]).astype(BF16)
        part = _dot(yn, wo_ref[sl, :])
        acc = part if acc is None else acc + part
    xn = x_ref[0] + g1_ref[0] * acc
    xo_ref[0] = xn
    fx = _rms(xn, n2_ref[...]) * (1.0 + sc2_ref[0]) + sh2_ref[0]
    fx_ref[0] = fx
    logits = lax.dot_general(fx, wr_ref[...], (((1,), (0,)), ((), ())),
                             precision=HIGHEST, preferred_element_type=F32)
    lane = lax.broadcasted_iota(jnp.int32, logits.shape, 1)
    logits = jnp.where(lane < N_EXPERTS, logits, NEG)
    e = jnp.exp(logits - jnp.max(logits, axis=-1, keepdims=True))
    aff_ref[0] = e / jnp.sum(e, axis=-1, keepdims=True)


def _mixer_out(x, g1, yc, ys, yf, ym, out_norm_g, w_out_b, norm2_g, sh2, sc2, w_router_p):
    b, n, d = x.shape
    tm = min(256, n)
    row = lambda bi, i: (bi, i, 0)
    per_b = lambda bi, i: (bi, 0, 0)
    const2 = lambda bi, i: (0, 0)
    return pl.pallas_call(
        _mixout_kernel,
        grid=(b, n // tm),
        in_specs=[
            pl.BlockSpec((1, tm, d), row),
            pl.BlockSpec((1, 1, d), per_b),
            pl.BlockSpec((1, tm, GROUP_W), row),
            pl.BlockSpec((1, tm, GROUP_W), row),
            pl.BlockSpec((1, tm, GROUP_W), row),
            pl.BlockSpec((1, tm, MLA_HEADS * LANES), row),
            pl.BlockSpec((1, N_GROUPS * GROUP_W), const2),
            pl.BlockSpec((N_GROUPS * GROUP_W, d), const2),
            pl.BlockSpec((1, d), const2),
            pl.BlockSpec((1, 1, d), per_b),
            pl.BlockSpec((1, 1, d), per_b),
            pl.BlockSpec((d, LANES), const2),
        ],
        out_specs=[pl.BlockSpec((1, tm, d), row), pl.BlockSpec((1, tm, d), row),
                   pl.BlockSpec((1, tm, LANES), row)],
        out_shape=[jax.ShapeDtypeStruct((b, n, d), F32), jax.ShapeDtypeStruct((b, n, d), F32),
                   jax.ShapeDtypeStruct((b, n, LANES), F32)],
        compiler_params=_cparams(("arbitrary", "arbitrary")),
        name="mixer_out",
    )(x, g1, yc, ys, yf, ym, out_norm_g, w_out_b, norm2_g, sh2, sc2, w_router_p)


def _ffn_kernel(idx_ref, g_ref, wg_ref, wu_ref, wd_ref, *rest, seg_rows, chunk, n_f, tf):
    n_src = len(seg_rows)
    srcs = rest[:n_src]
    y_ref = rest[n_src]
    stage_ref, xs_ref, act_ref, sem = rest[n_src + 1:]
    s = pl.program_id(1)

    @pl.when(s == 0)
    def _gather():
        base = 0
        for src, rows in zip(srcs, seg_rows):
            for c0 in range(0, rows, chunk):
                cn = min(chunk, rows - c0)
                off = base + c0

                def issue(j, carry, off=off, src=src):
                    r = idx_ref[0, 0, off + j]
                    pltpu.make_async_copy(src.at[pl.ds(r, 1), :], stage_ref.at[pl.ds(j, 1), :], sem).start()
                    return carry

                lax.fori_loop(0, cn, issue, 0)
                pltpu.make_async_copy(src.at[pl.ds(0, cn), :], stage_ref.at[pl.ds(0, cn), :], sem).wait()
                xs_ref[off:off + cn, :] = stage_ref[0:cn, :].astype(BF16)
            base += rows

    @pl.when(s < n_f)
    def _gate_up():
        xs = xs_ref[...]
        a = _dot(xs, wg_ref[0].astype(BF16))
        u = _dot(xs, wu_ref[0].astype(BF16))
        act_ref[s] = (a / (1.0 + jnp.exp(-a)) * u).astype(BF16)

    @pl.when(s >= n_f)
    def _down():
        acc = None
        for k in range(n_f):
            part = _dot(act_ref[k], wd_ref[0, k * tf:(k + 1) * tf, :].astype(BF16))
            acc = part if acc is None else acc + part
        y_ref[0] = acc * g_ref[0]


def _expert_ffn(idx_all, g_all, w_gate, w_up, w_down, srcs, seg_rows):
    e, _, r = idx_all.shape
    _, d, ff = w_gate.shape
    tf = min(256, ff)
    td = min(256, d)
    n_f = ff // tf
    n_d = d // td
    chunk = min(512, max(seg_rows))
    kern = functools.partial(_ffn_kernel, seg_rows=tuple(seg_rows), chunk=chunk, n_f=n_f, tf=tf)
    return pl.pallas_call(
        kern,
        grid=(e, n_f + n_d),
        in_specs=[
            pl.BlockSpec((1, 1, r), lambda ei, s: (ei, 0, 0), memory_space=pltpu.SMEM),
            pl.BlockSpec((1, r, 1), lambda ei, s: (ei, 0, 0)),
            pl.BlockSpec((1, d, tf), lambda ei, s: (ei, 0, jnp.minimum(s, n_f - 1))),
            pl.BlockSpec((1, d, tf), lambda ei, s: (ei, 0, jnp.minimum(s, n_f - 1))),
            pl.BlockSpec((1, ff, td), lambda ei, s: (ei, 0, jnp.maximum(s - n_f, 0))),
        ] + [pl.BlockSpec(memory_space=pl.ANY) for _ in srcs],
        out_specs=pl.BlockSpec((1, r, td), lambda ei, s: (ei, 0, jnp.maximum(s - n_f, 0))),
        out_shape=jax.ShapeDtypeStruct((e, r, d), F32),
        scratch_shapes=[
            pltpu.VMEM((chunk, d), F32),
            pltpu.VMEM((r, d), BF16),
            pltpu.VMEM((n_f, r, tf), BF16),
            pltpu.SemaphoreType.DMA(()),
        ],
        compiler_params=_cparams(("arbitrary", "arbitrary")),
        name="expert_ffn",
    )(idx_all, g_all, w_gate, w_up, w_down, *srcs)


def _final_norm_kernel(x_ref, g_ref, o_ref):
    o_ref[0] = _rms(x_ref[0], g_ref[...])


def _final_norm(x, g):
    b, n, d = x.shape
    tm = min(512, n)
    return pl.pallas_call(
        _final_norm_kernel,
        grid=(b, n // tm),
        in_specs=[pl.BlockSpec((1, tm, d), lambda bi, i: (bi, i, 0)), pl.BlockSpec((1, d), lambda bi, i: (0, 0))],
        out_specs=pl.BlockSpec((1, tm, d), lambda bi, i: (bi, i, 0)),
        out_shape=jax.ShapeDtypeStruct((b, n, d), F32),
        compiler_params=_cparams(("arbitrary", "arbitrary")),
        name="final_norm",
    )(x, g)


def _axial_tables(n_tok, rot_dim):
    rows = (jnp.arange(n_tok, dtype=jnp.int32) // GRID_W).astype(F32)
    cols = (jnp.arange(n_tok, dtype=jnp.int32) % GRID_W).astype(F32)
    n_freq = rot_dim // 4
    inv_freq = ROPE_THETA ** (-jnp.arange(n_freq, dtype=F32) / n_freq)
    ang = jnp.concatenate([rows[:, None] * inv_freq, cols[:, None] * inv_freq], axis=-1)
    return jnp.cos(ang), jnp.sin(ang)


def _rope_tables(n_tok, identity):
    if identity:
        one = jnp.ones((n_tok, LANES), F32)
        zero = jnp.zeros((n_tok, LANES), F32)
        return one, zero, one, zero
    cos_s, sin_s = _axial_tables(n_tok, SWA_HEAD_DIM)
    css = jnp.tile(jnp.concatenate([cos_s, cos_s], axis=-1), (1, 2))
    sns = jnp.tile(jnp.concatenate([-sin_s, sin_s], axis=-1), (1, 2))
    cos_m, sin_m = _axial_tables(n_tok, MLA_ROPE)
    ones = jnp.ones((n_tok, MLA_NOPE), F32)
    tail = LANES - MLA_NOPE - MLA_ROPE
    csm = jnp.concatenate([ones, cos_m, cos_m, jnp.ones((n_tok, tail), F32)], axis=-1)
    snm = jnp.concatenate([0.0 * ones, -sin_m, sin_m, jnp.zeros((n_tok, tail), F32)], axis=-1)
    return css, sns, csm, snm


def _mla_weight_slots(w_uq, w_ukv):
    dq = MLA_NOPE + MLA_ROPE
    wuq = jnp.pad(w_uq.reshape(MLA_Q_LORA, MLA_HEADS, dq), ((0, 0), (0, 0), (0, LANES - dq)))
    wkv = w_ukv.reshape(MLA_KV_LORA, MLA_HEADS, MLA_NOPE + MLA_V)
    wuk = jnp.pad(wkv[..., :MLA_NOPE], ((0, 0), (0, 0), (0, LANES - MLA_NOPE)))
    wv = wkv[..., MLA_NOPE:]
    even = jnp.pad(wv, ((0, 0), (0, 0), (0, LANES - MLA_V)))
    odd = jnp.pad(wv, ((0, 0), (0, 0), (LANES - MLA_V, 0)))
    is_even = (jnp.arange(MLA_HEADS) % 2 == 0)[None, :, None]
    wuv = jnp.where(is_even, even, odd)
    flat = lambda w: w.reshape(w.shape[0], MLA_HEADS * LANES).astype(BF16)
    return flat(wuq), flat(wuk), flat(wuv)


def _short_conv(cb, u, w):
    up = jnp.pad(u.astype(F32), ((0, 0), (1, 1), (0, 0)))
    y = up[:, :-2] * w[0] + up[:, 1:-1] * w[1] + up[:, 2:] * w[2]
    return (cb.astype(F32) * y).astype(BF16)


def _fourier(u):
    b, n, _ = u.shape
    uh = u.reshape(b, n, GROUP_W // 64, 64).astype(F32)
    y = jnp.fft.fft2(uh, axes=(1, 3), norm="ortho").real
    return y.reshape(b, n, GROUP_W).astype(F32)


def kernel(x, c, ctx, c_ctx, ada_w, ada_b, norm1_g, norm2_g, w_in, conv_w, swa_sink, mla_q_norm_g, mla_w_uq,
           mla_kv_norm_g, mla_w_ukv, out_norm_g, w_out, w_router, w_gate, w_up, w_down, final_norm_g):
    b, n, d = x.shape
    l = ctx.shape[1]
    depth = ada_w.shape[0]
    cap_x = EC_CAPACITY * n // N_EXPERTS
    cap_c = EC_CAPACITY * l // N_EXPERTS

    c_rows = jnp.concatenate([c, c_ctx[None, :], jnp.zeros((-(b + 1) % 8, d), F32)], axis=0)
    mods = _adaln(c_rows, ada_w, ada_b)
    rope_x = _rope_tables(n, identity=False)
    rope_c = _rope_tables(l, identity=True)

    for layer in range(depth):
        last = layer == depth - 1
        mx = mods[layer, :b].reshape(b, 1, 6, d)
        mc = jnp.broadcast_to(mods[layer, b].reshape(1, 1, 6, d), (b, 1, 6, d))
        sh1, sc1, g1, sh2, sc2, g2 = [mx[:, :, k] for k in range(6)]
        csh1, csc1, cg1, csh2, csc2, cg2 = [mc[:, :, k] for k in range(6)]

        w_in_p = jnp.pad(w_in[layer].astype(BF16), ((0, 0), (0, IN_COLS_PAD - IN_COLS)))
        wuq, wuk, wuv = _mla_weight_slots(mla_w_uq[layer], mla_w_ukv[layer])
        n1 = norm1_g[layer][None, :]
        qg = mla_q_norm_g[layer][None, :]
        kvg = mla_kv_norm_g[layer][None, :]
        px = _mixer_inputs(x, sh1, sc1, n1, w_in_p, qg, kvg, wuq, wuk, wuv, rope_x)
        pc = _mixer_inputs(ctx, csh1, csc1, n1, w_in_p, qg, kvg, wuq, wuk, wuv, rope_c)
        cb, u, sq, skp, svp, fu, mq, mk, mvp = px
        ccb, cu, csq, cskp, csvp, cfu, cmq, cmk, cmvp = pc

        sink = swa_sink[layer]
        og = out_norm_g[layer][None, :]
        wo = w_out[layer].astype(BF16)
        n2 = norm2_g[layer][None, :]
        wr = jnp.pad(w_router[layer], ((0, 0), (0, LANES - N_EXPERTS)))

        yx = (_short_conv(cb, u, conv_w[layer]), _swa(sink, sq, cskp, csvp, skp, svp),
              _fourier(fu), _mla(mq, cmk, cmvp, mk, mvp))
        x, fx, aff = _mixer_out(x, g1, *yx, og, wo, n2, sh2, sc2, wr)
        srcs = [fx.reshape(b * n, d)]
        seg_rows = [b * cap_x]
        gx, ix = lax.top_k(jnp.swapaxes(aff[..., :N_EXPERTS], 1, 2), cap_x)
        rows_x = ix + (jnp.arange(b, dtype=jnp.int32) * n)[:, None, None]
        idx_parts = [jnp.swapaxes(rows_x, 0, 1).reshape(N_EXPERTS, b * cap_x)]
        g_parts = [jnp.swapaxes(gx, 0, 1).reshape(N_EXPERTS, b * cap_x)]
        if not last:
            yc = (_short_conv(ccb, cu, conv_w[layer]), _swa(sink, csq, cskp, csvp),
                  _fourier(cfu), _mla(cmq, cmk, cmvp))
            ctx, fc, affc = _mixer_out(ctx, cg1, *yc, og, wo, n2, csh2, csc2, wr)
            srcs.append(fc.reshape(b * l, d))
            seg_rows.append(b * cap_c)
            gc, ic = lax.top_k(jnp.swapaxes(affc[..., :N_EXPERTS], 1, 2), cap_c)
            rows_c = ic + (jnp.arange(b, dtype=jnp.int32) * l)[:, None, None]
            idx_parts.append(jnp.swapaxes(rows_c, 0, 1).reshape(N_EXPERTS, b * cap_c))
            g_parts.append(jnp.swapaxes(gc, 0, 1).reshape(N_EXPERTS, b * cap_c))
        idx_all = jnp.concatenate(idx_parts, axis=1).astype(jnp.int32)[:, None, :]
        g_all = jnp.concatenate(g_parts, axis=1)[:, :, None]
        y = _expert_ffn(idx_all, g_all, w_gate[layer], w_up[layer], w_down[layer], srcs, seg_rows)

        yx_rows = y[:, :b * cap_x].reshape(N_EXPERTS * b * cap_x, d)
        upd = jnp.zeros((b * n, d), F32).at[idx_parts[0].reshape(-1)].add(yx_rows).reshape(b, n, d)
        x = x + g2 * upd
        if not last:
            yc_rows = y[:, b * cap_x:].reshape(N_EXPERTS * b * cap_c, d)
            updc = jnp.zeros((b * l, d), F32).at[idx_parts[1].reshape(-1)].add(yc_rows).reshape(b, l, d)
            ctx = ctx + cg2 * updc
    return _final_norm(x, final_norm_g[None, :])
```

```python
import functools
import math

import jax
import jax.numpy as jnp
from jax import lax
from jax.experimental import pallas as pl
from jax.experimental.pallas import tpu as pltpu

F32 = jnp.float32
BF16 = jnp.bfloat16
HIGHEST = lax.Precision.HIGHEST

GRID_W = 64
GROUP_W = 512
N_GROUPS = 4
SWA_HEAD_DIM = 64
SWA_HEADS = 8
SWA_KV_HEADS = 2
SWA_BLOCK = 128
MLA_HEADS = 8
MLA_NOPE = 64
MLA_ROPE = 32
MLA_V = 64
MLA_Q_LORA = 512
MLA_KV_LORA = 256
N_EXPERTS = 16
EC_CAPACITY = 2
ROPE_THETA = 10000.0
EPS = 1e-6
IN_COLS = 3616
IN_COLS_PAD = 3712
LANES = 128
SUBLANES = 8
NEG = -1e30

VMEM_LIMIT = 60 * 1024 * 1024


def _cparams(sem):
    return pltpu.CompilerParams(dimension_semantics=sem, vmem_limit_bytes=VMEM_LIMIT)


def _dot(a, b):
    return jnp.dot(a, b, preferred_element_type=F32)


def _dot_t(a, b):
    return lax.dot_general(a, b, (((1,), (1,)), ((), ())), preferred_element_type=F32)


def _rms(x, g):
    return x * lax.rsqrt(jnp.mean(x * x, axis=-1, keepdims=True) + EPS) * g


def _adaln_kernel(c_ref, w_ref, b_ref, o_ref):
    c = c_ref[...]
    s = c / (1.0 + jnp.exp(-c))
    o_ref[0] = lax.dot_general(s, w_ref[0], (((1,), (0,)), ((), ())),
                               precision=HIGHEST, preferred_element_type=F32) + b_ref[0]


def _adaln(c_rows, ada_w, ada_b):
    depth, d, d6 = ada_w.shape
    tn = 1024 if d6 % 1024 == 0 else d6
    rows = c_rows.shape[0]
    return pl.pallas_call(
        _adaln_kernel,
        grid=(depth, d6 // tn),
        in_specs=[
            pl.BlockSpec((rows, d), lambda l, j: (0, 0)),
            pl.BlockSpec((1, d, tn), lambda l, j: (l, 0, j)),
            pl.BlockSpec((1, 1, tn), lambda l, j: (l, 0, j)),
        ],
        out_specs=pl.BlockSpec((1, rows, tn), lambda l, j: (l, 0, j)),
        out_shape=jax.ShapeDtypeStruct((depth, rows, d6), F32),
        compiler_params=_cparams(("arbitrary", "arbitrary")),
        name="adaln",
    )(c_rows, ada_w, ada_b.reshape(depth, 1, d6))


def _swap_halves(x, width):
    lane = lax.broadcasted_iota(jnp.int32, x.shape, 1)
    first = (lane % (2 * width)) < width
    return jnp.where(first, pltpu.roll(x, LANES - width, 1), pltpu.roll(x, width, 1))


def _mixin_kernel(x_ref, sh_ref, sc_ref, g_ref, w_ref, qg_ref, kvg_ref, wuq_ref, wuk_ref, wuv_ref,
                  css_ref, sns_ref, csm_ref, snm_ref,
                  cb_ref, u_ref, sq_ref, skp_ref, svp_ref, fu_ref, mq_ref, mk_ref, mvp_ref):
    x = x_ref[0]
    h = _rms(x, g_ref[...]) * (1.0 + sc_ref[0]) + sh_ref[0]
    hb = h.astype(BF16)

    pc = _dot(hb, w_ref[:, 0:3 * GROUP_W])
    cb_ref[0] = pc[:, 0:GROUP_W].astype(BF16)
    u_ref[0] = (pc[:, GROUP_W:2 * GROUP_W] * pc[:, 2 * GROUP_W:3 * GROUP_W]).astype(BF16)

    c0 = 3 * GROUP_W
    ps = _dot(hb, w_ref[:, c0:c0 + 768])
    css = css_ref[...]
    sns = sns_ref[...]
    qscale = SWA_HEAD_DIM ** -0.5
    for c in range(4):
        t = ps[:, c * LANES:(c + 1) * LANES]
        t = (t * css + _swap_halves(t, SWA_HEAD_DIM // 2) * sns) * qscale
        sq_ref[0, :, c * LANES:(c + 1) * LANES] = t.astype(BF16)
    sk = ps[:, 512:640]
    sk = sk * css + _swap_halves(sk, SWA_HEAD_DIM // 2) * sns
    sv = ps[:, 640:768]
    lane = lax.broadcasted_iota(jnp.int32, sk.shape, 1)
    lo = lane < SWA_HEAD_DIM
    for src, dst in ((sk, skp_ref), (sv, svp_ref)):
        sw = pltpu.roll(src, SWA_HEAD_DIM, 1)
        zero = jnp.zeros_like(src)
        dst[0, :, 0:128] = jnp.where(lo, src, zero).astype(BF16)
        dst[0, :, 128:256] = jnp.where(lo, zero, sw).astype(BF16)
        dst[0, :, 256:384] = jnp.where(lo, sw, zero).astype(BF16)
        dst[0, :, 384:512] = jnp.where(lo, zero, src).astype(BF16)

    c0 += 768
    fu_ref[0] = _dot(hb, w_ref[:, c0:c0 + GROUP_W]).astype(BF16)

    c0 += GROUP_W
    pm = _dot(hb, w_ref[:, c0:c0 + 896])
    qn = _rms(pm[:, 0:MLA_Q_LORA], qg_ref[...]).astype(BF16)
    kvn = _rms(pm[:, MLA_Q_LORA:MLA_Q_LORA + MLA_KV_LORA], kvg_ref[...]).astype(BF16)
    csm = csm_ref[...]
    snm = snm_ref[...]

    def rope_m(t):
        lane_m = lax.broadcasted_iota(jnp.int32, t.shape, 1)
        sw = jnp.where(lane_m < MLA_NOPE + MLA_ROPE // 2,
                       pltpu.roll(t, LANES - MLA_ROPE // 2, 1), pltpu.roll(t, MLA_ROPE // 2, 1))
        return t * csm + sw * snm

    q = _dot(qn, wuq_ref[...])
    mscale = (MLA_NOPE + MLA_ROPE) ** -0.5
    kpe = rope_m(pltpu.roll(pm[:, 768:896], MLA_NOPE, 1))
    kn = _dot(kvn, wuk_ref[...])
    for hd in range(MLA_HEADS):
        sl = slice(hd * LANES, (hd + 1) * LANES)
        mq_ref[0, :, sl] = (rope_m(q[:, sl]) * mscale).astype(BF16)
        mk_ref[0, :, sl] = (kn[:, sl] + kpe).astype(BF16)
    mvp_ref[0] = _dot(kvn, wuv_ref[...]).astype(BF16)


def _mixer_inputs(x, shift, scale, norm_g, w_in_p, qg, kvg, wuq, wuk, wuv, rope_tabs):
    b, n, d = x.shape
    tm = min(256, n)
    css, sns, csm, snm = rope_tabs
    row = lambda bi, i: (bi, i, 0)
    const2 = lambda bi, i: (0, 0)
    tab = lambda bi, i: (i, 0)
    outs = [jax.ShapeDtypeStruct((b, n, w), BF16) for w in (512, 512, 512, 512, 512, 512, 1024, 1024, 1024)]
    return pl.pallas_call(
        _mixin_kernel,
        grid=(b, n // tm),
        in_specs=[
            pl.BlockSpec((1, tm, d), row),
            pl.BlockSpec((1, 1, d), lambda bi, i: (bi, 0, 0)),
            pl.BlockSpec((1, 1, d), lambda bi, i: (bi, 0, 0)),
            pl.BlockSpec((1, d), const2),
            pl.BlockSpec((d, IN_COLS_PAD), const2),
            pl.BlockSpec((1, MLA_Q_LORA), const2),
            pl.BlockSpec((1, MLA_KV_LORA), const2),
            pl.BlockSpec((MLA_Q_LORA, 1024), const2),
            pl.BlockSpec((MLA_KV_LORA, 1024), const2),
            pl.BlockSpec((MLA_KV_LORA, 1024), const2),
            pl.BlockSpec((tm, LANES), tab),
            pl.BlockSpec((tm, LANES), tab),
            pl.BlockSpec((tm, LANES), tab),
            pl.BlockSpec((tm, LANES), tab),
        ],
        out_specs=[pl.BlockSpec((1, tm, o.shape[-1]), row) for o in outs],
        out_shape=outs,
        compiler_params=_cparams(("arbitrary", "arbitrary")),
        name="mixer_inputs",
    )(x, shift, scale, norm_g, w_in_p, qg, kvg, wuq, wuk, wuv, css, sns, csm, snm)


def _swa_kernel(sink_ref, q_ref, kc_ref, vc_ref, *rest, has_band, n_blocks):
    if has_band:
        kp_ref, vp_ref, o_ref = rest
    else:
        (o_ref,) = rest
    i = pl.program_id(1)
    t = SWA_BLOCK
    if has_band:
        ql = lax.broadcasted_iota(jnp.int32, (t, t), 0)
        kl = lax.broadcasted_iota(jnp.int32, (t, t), 1)
        mask_prev = (kl >= ql) & (i > 0)
        mask_next = (kl <= ql) & (i < n_blocks - 1)
        r_prev = pl.multiple_of(jnp.maximum(i - 1, 0) * t, t)
        r_cur = pl.multiple_of(i * t, t)
        r_next = pl.multiple_of(jnp.minimum(i + 1, n_blocks - 1) * t, t)
    for c in range(4):
        qc = q_ref[0, :, c * LANES:(c + 1) * LANES]
        acc = jnp.zeros((t, LANES), F32)
        for hh in range(2):
            head = 2 * c + hh
            slot = 2 * (c // 2) + hh
            sl = slice(slot * LANES, (slot + 1) * LANES)
            s_ctx = _dot_t(qc, kc_ref[0, :, sl])
            sink = sink_ref[head]
            m = jnp.maximum(jnp.max(s_ctx, axis=-1, keepdims=True), sink)
            if has_band:
                s_p = jnp.where(mask_prev, _dot_t(qc, kp_ref[0, pl.ds(r_prev, t), sl]), NEG)
                s_c = _dot_t(qc, kp_ref[0, pl.ds(r_cur, t), sl])
                s_n = jnp.where(mask_next, _dot_t(qc, kp_ref[0, pl.ds(r_next, t), sl]), NEG)
                m = jnp.maximum(m, jnp.max(jnp.maximum(jnp.maximum(s_p, s_c), s_n), axis=-1, keepdims=True))
            p_ctx = jnp.exp(s_ctx - m)
            den = jnp.sum(p_ctx, axis=-1, keepdims=True) + jnp.exp(sink - m)
            o = _dot(p_ctx.astype(BF16), vc_ref[0, :, sl])
            if has_band:
                p_p = jnp.exp(s_p - m)
                p_c = jnp.exp(s_c - m)
                p_n = jnp.exp(s_n - m)
                den = den + jnp.sum(p_p + p_c + p_n, axis=-1, keepdims=True)
                o = o + _dot(p_p.astype(BF16), vp_ref[0, pl.ds(r_prev, t), sl])
                o = o + _dot(p_c.astype(BF16), vp_ref[0, pl.ds(r_cur, t), sl])
                o = o + _dot(p_n.astype(BF16), vp_ref[0, pl.ds(r_next, t), sl])
            acc = acc + o / den
        o_ref[0, :, c * LANES:(c + 1) * LANES] = acc.astype(BF16)


def _swa(sink, q, kc, vc, kp=None, vp=None):
    b, n, _ = q.shape
    l = kc.shape[1]
    t = SWA_BLOCK
    has_band = kp is not None
    blk = lambda bi, i: (bi, i, 0)
    whole = lambda bi, i: (bi, 0, 0)
    in_specs = [
        pl.BlockSpec(memory_space=pltpu.SMEM),
        pl.BlockSpec((1, t, 512), blk),
        pl.BlockSpec((1, l, 512), whole),
        pl.BlockSpec((1, l, 512), whole),
    ]
    args = [sink, q, kc, vc]
    if has_band:
        in_specs += [pl.BlockSpec((1, n, 512), whole), pl.BlockSpec((1, n, 512), whole)]
        args += [kp, vp]
    return pl.pallas_call(
        functools.partial(_swa_kernel, has_band=has_band, n_blocks=n // t),
        grid=(b, n // t),
        in_specs=in_specs,
        out_specs=pl.BlockSpec((1, t, 512), blk),
        out_shape=jax.ShapeDtypeStruct((b, n, 512), BF16),
        compiler_params=_cparams(("arbitrary", "arbitrary")),
        name="swa_latent" if has_band else "swa_context",
    )(*args)


def _mla_kernel(q_ref, kc_ref, vc_ref, *rest, has_lat):
    if has_lat:
        k_ref, v_ref, o_ref = rest
    else:
        (o_ref,) = rest
    q = q_ref[0]
    s_ctx = _dot_t(q, kc_ref[0])
    m = jnp.max(s_ctx, axis=-1, keepdims=True)
    if has_lat:
        s_lat = _dot_t(q, k_ref[0])
        m = jnp.maximum(m, jnp.max(s_lat, axis=-1, keepdims=True))
    p_ctx = jnp.exp(s_ctx - m)
    den = jnp.sum(p_ctx, axis=-1, keepdims=True)
    o = _dot(p_ctx.astype(BF16), vc_ref[0])
    if has_lat:
        p_lat = jnp.exp(s_lat - m)
        den = den + jnp.sum(p_lat, axis=-1, keepdims=True)
        o = o + _dot(p_lat.astype(BF16), v_ref[0])
    o_ref[0] = (o / den).astype(BF16)


def _mla(q, kc, vc, k=None, v=None):
    b, n, _ = q.shape
    l = kc.shape[1]
    tq = min(256, n)
    has_lat = k is not None
    in_specs = [
        pl.BlockSpec((1, tq, LANES), lambda bi, h, i: (bi, i, h)),
        pl.BlockSpec((1, l, LANES), lambda bi, h, i: (bi, 0, h)),
        pl.BlockSpec((1, l, LANES), lambda bi, h, i: (bi, 0, h)),
    ]
    args = [q, kc, vc]
    if has_lat:
        in_specs += [pl.BlockSpec((1, n, LANES), lambda bi, h, i: (bi, 0, h)),
                     pl.BlockSpec((1, n, LANES), lambda bi, h, i: (bi, 0, h))]
        args += [k, v]
    return pl.pallas_call(
        functools.partial(_mla_kernel, has_lat=has_lat),
        grid=(b, MLA_HEADS, n // tq),
        in_specs=in_specs,
        out_specs=pl.BlockSpec((1, tq, LANES), lambda bi, h, i: (bi, i, h)),
        out_shape=jax.ShapeDtypeStruct((b, n, MLA_HEADS * LANES), BF16),
        compiler_params=_cparams(("arbitrary", "arbitrary", "arbitrary")),
        name="mla_latent" if has_lat else "mla_context",
    )(*args)


def _mixout_kernel(x_ref, g1_ref, yc_ref, ys_ref, yf_ref, ym_ref, og_ref, wo_ref,
                   n2_ref, sh2_ref, sc2_ref, wr_ref, xo_ref, fx_ref, aff_ref):
    ym = ym_ref[0].astype(F32)
    mla = jnp.concatenate(
        [ym[:, (2 * c) * LANES:(2 * c + 1) * LANES] + ym[:, (2 * c + 1) * LANES:(2 * c + 2) * LANES]
         for c in range(4)], axis=-1)
    groups = (yc_ref[0].astype(F32), ys_ref[0].astype(F32), yf_ref[0].astype(F32), mla)
    acc = None
    for gi, y in enumerate(groups):
        sl = slice(gi * GROUP_W, (gi + 1) * GROUP_W)
        yn = _rms(y, og_ref[:, sl]).astype(BF16)
        part = _dot(yn, wo_ref[sl, :])
        acc = part if acc is None else acc + part
    xn = x_ref[0] + g1_ref[0] * acc
    xo_ref[0] = xn
    fx = _rms(xn, n2_ref[...]) * (1.0 + sc2_ref[0]) + sh2_ref[0]
    fx_ref[0] = fx
    logits = lax.dot_general(fx, wr_ref[...], (((1,), (0,)), ((), ())),
                             precision=HIGHEST, preferred_element_type=F32)
    lane = lax.broadcasted_iota(jnp.int32, logits.shape, 1)
    logits = jnp.where(lane < N_EXPERTS, logits, NEG)
    e = jnp.exp(logits - jnp.max(logits, axis=-1, keepdims=True))
    aff_ref[0] = e / jnp.sum(e, axis=-1, keepdims=True)


def _mixer_out(x, g1, yc, ys, yf, ym, out_norm_g, w_out_b, norm2_g, sh2, sc2, w_router_p):
    b, n, d = x.shape
    tm = min(256, n)
    row = lambda bi, i: (bi, i, 0)
    per_b = lambda bi, i: (bi, 0, 0)
    const2 = lambda bi, i: (0, 0)
    return pl.pallas_call(
        _mixout_kernel,
        grid=(b, n // tm),
        in_specs=[
            pl.BlockSpec((1, tm, d), row),
            pl.BlockSpec((1, 1, d), per_b),
            pl.BlockSpec((1, tm, GROUP_W), row),
            pl.BlockSpec((1, tm, GROUP_W), row),
            pl.BlockSpec((1, tm, GROUP_W), row),
            pl.BlockSpec((1, tm, MLA_HEADS * LANES), row),
            pl.BlockSpec((1, N_GROUPS * GROUP_W), const2),
            pl.BlockSpec((N_GROUPS * GROUP_W, d), const2),
            pl.BlockSpec((1, d), const2),
            pl.BlockSpec((1, 1, d), per_b),
            pl.BlockSpec((1, 1, d), per_b),
            pl.BlockSpec((d, LANES), const2),
        ],
        out_specs=[pl.BlockSpec((1, tm, d), row), pl.BlockSpec((1, tm, d), row),
                   pl.BlockSpec((1, tm, LANES), row)],
        out_shape=[jax.ShapeDtypeStruct((b, n, d), F32), jax.ShapeDtypeStruct((b, n, d), F32),
                   jax.ShapeDtypeStruct((b, n, LANES), F32)],
        compiler_params=_cparams(("arbitrary", "arbitrary")),
        name="mixer_out",
    )(x, g1, yc, ys, yf, ym, out_norm_g, w_out_b, norm2_g, sh2, sc2, w_router_p)


def _ffn_kernel(idx_ref, wg_ref, wu_ref, wd_ref, *rest, seg_rows, chunk, n_f, tf):
    n_src = len(seg_rows)
    srcs = rest[:n_src]
    y_ref = rest[n_src]
    stage_ref, xs_ref, act_ref, sem = rest[n_src + 1:]
    s = pl.program_id(1)

    @pl.when(s == 0)
    def _gather():
        base = 0
        for src, rows in zip(srcs, seg_rows):
            for c0 in range(0, rows, chunk):
                cn = min(chunk, rows - c0)
                off = base + c0

                def issue(j, carry, off=off, src=src):
                    r = idx_ref[0, 0, off + j]
                    pltpu.make_async_copy(src.at[pl.ds(r, 1), :], stage_ref.at[pl.ds(j, 1), :], sem).start()
                    return carry

                lax.fori_loop(0, cn, issue, 0)
                pltpu.make_async_copy(src.at[pl.ds(0, cn), :], stage_ref.at[pl.ds(0, cn), :], sem).wait()
                xs_ref[off:off + cn, :] = stage_ref[0:cn, :].astype(BF16)
            base += rows

    @pl.when(s < n_f)
    def _gate_up():
        xs = xs_ref[...]
        a = _dot(xs, wg_ref[0].astype(BF16))
        u = _dot(xs, wu_ref[0].astype(BF16))
        act_ref[s] = (a / (1.0 + jnp.exp(-a)) * u).astype(BF16)

    @pl.when(s >= n_f)
    def _down():
        acc = None
        for k in range(n_f):
            part = _dot(act_ref[k], wd_ref[0, k * tf:(k + 1) * tf, :].astype(BF16))
            acc = part if acc is None else acc + part
        y_ref[0] = acc


def _expert_ffn(idx_all, w_gate, w_up, w_down, srcs, seg_rows):
    e, _, r = idx_all.shape
    _, d, ff = w_gate.shape
    tf = min(256, ff)
    td = min(256, d)
    n_f = ff // tf
    n_d = d // td
    chunk = min(512, max(seg_rows))
    kern = functools.partial(_ffn_kernel, seg_rows=tuple(seg_rows), chunk=chunk, n_f=n_f, tf=tf)
    return pl.pallas_call(
        kern,
        grid=(e, n_f + n_d),
        in_specs=[
            pl.BlockSpec((1, 1, r), lambda ei, s: (ei, 0, 0), memory_space=pltpu.SMEM),
            pl.BlockSpec((1, d, tf), lambda ei, s: (ei, 0, jnp.minimum(s, n_f - 1))),
            pl.BlockSpec((1, d, tf), lambda ei, s: (ei, 0, jnp.minimum(s, n_f - 1))),
            pl.BlockSpec((1, ff, td), lambda ei, s: (ei, 0, jnp.maximum(s - n_f, 0))),
        ] + [pl.BlockSpec(memory_space=pl.ANY) for _ in srcs],
        out_specs=pl.BlockSpec((1, r, td), lambda ei, s: (ei, 0, jnp.maximum(s - n_f, 0))),
        out_shape=jax.ShapeDtypeStruct((e, r, d), F32),
        scratch_shapes=[
            pltpu.VMEM((chunk, d), F32),
            pltpu.VMEM((r, d), BF16),
            pltpu.VMEM((n_f, r, tf), BF16),
            pltpu.SemaphoreType.DMA(()),
        ],
        compiler_params=_cparams(("arbitrary", "arbitrary")),
        name="expert_ffn",
    )(idx_all, w_gate, w_up, w_down, *srcs)


def _select_kernel(aff_ref, w_ref, idx_ref, bnd_ref, pos_ref, post_ref, *, cap, tile, jchunk):
    n = aff_ref.shape[1]
    bits = pltpu.bitcast(aff_ref[0], jnp.int32)

    def search(it, thr):
        cand = thr | jnp.left_shift(jnp.int32(1), 30 - it)
        cnt = jnp.sum(jnp.where(bits >= cand, 1, 0), axis=0, keepdims=True)
        return jnp.where(cnt >= cap, cand, thr)

    thr = lax.fori_loop(0, 31, search, jnp.zeros((1, LANES), jnp.int32))
    need = (cap - jnp.sum(jnp.where(bits > thr, 1, 0), axis=0, keepdims=True)).astype(F32)

    r = lax.broadcasted_iota(jnp.int32, (tile, tile), 0)
    c = lax.broadcasted_iota(jnp.int32, (tile, tile), 1)
    tri = jnp.where(c <= r, 1.0, 0.0).astype(BF16)

    def chunk(k, carry):
        ceq, cpos = carry
        r0 = pl.multiple_of(k * tile, tile)
        a = aff_ref[0, pl.ds(r0, tile), :]
        bt = pltpu.bitcast(a, jnp.int32)
        eq = bt == thr
        rank = _dot(tri, jnp.where(eq, 1.0, 0.0).astype(BF16)) + ceq
        sel = jnp.where((bt > thr) | (eq & (rank <= need)), 1.0, 0.0)
        pos = _dot(tri, sel.astype(BF16)) + cpos
        pos_ref[pl.ds(r0, tile), :] = pos
        w_ref[0, pl.ds(r0, tile), :] = sel * a
        cpos = pos[tile - 1:tile, :]
        bnd_ref[0, pl.ds(k, 1), :] = cpos.astype(jnp.int32)
        return rank[tile - 1:tile, :], cpos

    zero = jnp.zeros((1, LANES), F32)
    lax.fori_loop(0, n // tile, chunk, (zero, zero))

    for k in range(n // tile):
        post_ref[:, k * tile:(k + 1) * tile] = pos_ref[k * tile:(k + 1) * tile, :].T

    def per_expert(e, carry):
        row = post_ref[pl.ds(e, 1), :]

        def per_j(jc, carry_j):
            j0 = pl.multiple_of(jc * jchunk, jchunk)
            jv = (lax.broadcasted_iota(jnp.int32, (jchunk, n), 0) + j0).astype(F32)
            cnt = jnp.sum(jnp.where(row <= jv, 1.0, 0.0), axis=-1, keepdims=True)
            idx_ref[0, e, pl.ds(j0, jchunk), :] = cnt.astype(jnp.int32)
            return carry_j

        return lax.fori_loop(0, cap // jchunk, per_j, carry)

    lax.fori_loop(0, N_EXPERTS, per_expert, 0)


def _select(aff, cap, tile):
    b, n, _ = aff.shape
    jchunk = min(64, cap)
    kern = functools.partial(_select_kernel, cap=cap, tile=tile, jchunk=jchunk)
    return pl.pallas_call(
        kern,
        grid=(b,),
        in_specs=[pl.BlockSpec((1, n, LANES), lambda bi: (bi, 0, 0))],
        out_specs=[pl.BlockSpec((1, n, LANES), lambda bi: (bi, 0, 0)),
                   pl.BlockSpec((1, N_EXPERTS, cap, 1), lambda bi: (bi, 0, 0, 0)),
                   pl.BlockSpec((1, n // tile, LANES), lambda bi: (bi, 0, 0))],
        out_shape=[jax.ShapeDtypeStruct((b, n, LANES), F32),
                   jax.ShapeDtypeStruct((b, N_EXPERTS, cap, 1), jnp.int32),
                   jax.ShapeDtypeStruct((b, n // tile, LANES), jnp.int32)],
        scratch_shapes=[pltpu.VMEM((n, LANES), F32), pltpu.VMEM((LANES, n), F32)],
        compiler_params=_cparams(("arbitrary",)),
        name="select",
    )(aff)


def _combine_kernel(idx_ref, bnd_ref, x_ref, g2_ref, w_ref, *rest, tile, cap, row0, rows_per_expert, final_norm):
    if final_norm:
        fg_ref, y_hbm, o_ref, buf, sem = rest
    else:
        y_hbm, o_ref, buf, sem = rest
    b = pl.program_id(0)
    i = pl.program_id(1)

    @pl.when((b == 0) & (i == 0))
    def _init():
        buf[...] = jnp.zeros(buf.shape, F32)

    t0 = i * tile
    total = jnp.int32(0)
    for e in range(N_EXPERTS):
        lo = bnd_ref[0, i, e]
        hi = bnd_ref[0, i + 1, e]
        src0 = e * rows_per_expert + row0 + b * cap

        def issue(j, carry, e=e, src0=src0):
            t = idx_ref[0, e, j] - t0
            pltpu.make_async_copy(y_hbm.at[pl.ds(src0 + j, 1), :], buf.at[pl.ds(e * tile + t, 1), :], sem).start()
            return carry

        lax.fori_loop(lo, hi, issue, 0)
        total = total + (hi - lo)

    pad = (-total) & (SUBLANES - 1)

    def issue_pad(k, carry):
        pltpu.make_async_copy(y_hbm.at[pl.ds(k, 1), :], buf.at[pl.ds(N_EXPERTS * tile + k, 1), :], sem).start()
        return carry

    lax.fori_loop(0, pad, issue_pad, 0)
    padded = pl.multiple_of(total + pad, SUBLANES)

    @pl.when(padded > 0)
    def _wait():
        pltpu.make_async_copy(y_hbm.at[pl.ds(0, padded), :], buf.at[pl.ds(0, padded), :], sem).wait()

    w = w_ref[0]
    acc = None
    for e in range(N_EXPERTS):
        part = buf[e * tile:(e + 1) * tile, :] * w[:, e:e + 1]
        acc = part if acc is None else acc + part
    out = x_ref[0] + g2_ref[0] * acc
    if final_norm:
        out = _rms(out, fg_ref[...])
    o_ref[0] = out


def _combine(idx, bnd, x, g2, w, y_flat, tile, row0, rows_per_expert, final_g=None):
    b, n, d = x.shape
    cap = idx.shape[-1]
    final_norm = final_g is not None
    kern = functools.partial(_combine_kernel, tile=tile, cap=cap, row0=row0,
                             rows_per_expert=rows_per_expert, final_norm=final_norm)
    row = lambda bi, i: (bi, i, 0)
    per_b = lambda bi, i: (bi, 0, 0)
    in_specs = [
        pl.BlockSpec((1, N_EXPERTS, cap), per_b, memory_space=pltpu.SMEM),
        pl.BlockSpec((1, n // tile + 1, N_EXPERTS), per_b, memory_space=pltpu.SMEM),
        pl.BlockSpec((1, tile, d), row),
        pl.BlockSpec((1, 1, d), per_b),
        pl.BlockSpec((1, tile, LANES), row),
    ]
    args = [idx, bnd, x, g2, w]
    if final_norm:
        in_specs.append(pl.BlockSpec((1, d), lambda bi, i: (0, 0)))
        args.append(final_g)
    in_specs.append(pl.BlockSpec(memory_space=pl.ANY))
    args.append(y_flat)
    return pl.pallas_call(
        kern,
        grid=(b, n // tile),
        in_specs=in_specs,
        out_specs=pl.BlockSpec((1, tile, d), row),
        out_shape=jax.ShapeDtypeStruct((b, n, d), F32),
        scratch_shapes=[pltpu.VMEM((N_EXPERTS * tile + SUBLANES, d), F32), pltpu.SemaphoreType.DMA(())],
        compiler_params=_cparams(("arbitrary", "arbitrary")),
        name="combine",
    )(*args)


def _route(aff, cap, tile, row_stride):
    b = aff.shape[0]
    w, idx, bnd = _select(aff, cap, tile)
    idx = idx.reshape(b, N_EXPERTS, cap)
    bnd = jnp.pad(bnd[:, :, :N_EXPERTS], ((0, 0), (1, 0), (0, 0)))
    rows = idx + (jnp.arange(b, dtype=jnp.int32) * row_stride)[:, None, None]
    rows = jnp.swapaxes(rows, 0, 1).reshape(N_EXPERTS, b * cap)
    return w, idx, bnd, rows


def _axial_tables(n_tok, rot_dim):
    rows = (jnp.arange(n_tok, dtype=jnp.int32) // GRID_W).astype(F32)
    cols = (jnp.arange(n_tok, dtype=jnp.int32) % GRID_W).astype(F32)
    n_freq = rot_dim // 4
    inv_freq = ROPE_THETA ** (-jnp.arange(n_freq, dtype=F32) / n_freq)
    ang = jnp.concatenate([rows[:, None] * inv_freq, cols[:, None] * inv_freq], axis=-1)
    return jnp.cos(ang), jnp.sin(ang)


def _rope_tables(n_tok, identity):
    if identity:
        one = jnp.ones((n_tok, LANES), F32)
        zero = jnp.zeros((n_tok, LANES), F32)
        return one, zero, one, zero
    cos_s, sin_s = _axial_tables(n_tok, SWA_HEAD_DIM)
    css = jnp.tile(jnp.concatenate([cos_s, cos_s], axis=-1), (1, 2))
    sns = jnp.tile(jnp.concatenate([-sin_s, sin_s], axis=-1), (1, 2))
    cos_m, sin_m = _axial_tables(n_tok, MLA_ROPE)
    ones = jnp.ones((n_tok, MLA_NOPE), F32)
    tail = LANES - MLA_NOPE - MLA_ROPE
    csm = jnp.concatenate([ones, cos_m, cos_m, jnp.ones((n_tok, tail), F32)], axis=-1)
    snm = jnp.concatenate([0.0 * ones, -sin_m, sin_m, jnp.zeros((n_tok, tail), F32)], axis=-1)
    return css, sns, csm, snm


def _mla_weight_slots(w_uq, w_ukv):
    dq = MLA_NOPE + MLA_ROPE
    wuq = jnp.pad(w_uq.reshape(MLA_Q_LORA, MLA_HEADS, dq), ((0, 0), (0, 0), (0, LANES - dq)))
    wkv = w_ukv.reshape(MLA_KV_LORA, MLA_HEADS, MLA_NOPE + MLA_V)
    wuk = jnp.pad(wkv[..., :MLA_NOPE], ((0, 0), (0, 0), (0, LANES - MLA_NOPE)))
    wv = wkv[..., MLA_NOPE:]
    even = jnp.pad(wv, ((0, 0), (0, 0), (0, LANES - MLA_V)))
    odd = jnp.pad(wv, ((0, 0), (0, 0), (LANES - MLA_V, 0)))
    is_even = (jnp.arange(MLA_HEADS) % 2 == 0)[None, :, None]
    wuv = jnp.where(is_even, even, odd)
    flat = lambda w: w.reshape(w.shape[0], MLA_HEADS * LANES).astype(BF16)
    return flat(wuq), flat(wuk), flat(wuv)


def _short_conv(cb, u, w):
    up = jnp.pad(u.astype(F32), ((0, 0), (1, 1), (0, 0)))
    y = up[:, :-2] * w[0] + up[:, 1:-1] * w[1] + up[:, 2:] * w[2]
    return (cb.astype(F32) * y).astype(BF16)


def _fourier(u):
    b, n, _ = u.shape
    uh = u.reshape(b, n, GROUP_W // 64, 64).astype(F32)
    y = jnp.fft.fft2(uh, axes=(1, 3), norm="ortho").real
    return y.reshape(b, n, GROUP_W).astype(F32)


def kernel(x, c, ctx, c_ctx, ada_w, ada_b, norm1_g, norm2_g, w_in, conv_w, swa_sink, mla_q_norm_g, mla_w_uq,
           mla_kv_norm_g, mla_w_ukv, out_norm_g, w_out, w_router, w_gate, w_up, w_down, final_norm_g):
    b, n, d = x.shape
    l = ctx.shape[1]
    depth = ada_w.shape[0]
    cap_x = EC_CAPACITY * n // N_EXPERTS
    cap_c = EC_CAPACITY * l // N_EXPERTS
    tile_x = min(128, n)
    tile_c = min(128, l)

    c_rows = jnp.concatenate([c, c_ctx[None, :], jnp.zeros((-(b + 1) % 8, d), F32)], axis=0)
    mods = _adaln(c_rows, ada_w, ada_b)
    rope_x = _rope_tables(n, identity=False)
    rope_c = _rope_tables(l, identity=True)

    for layer in range(depth):
        last = layer == depth - 1
        mx = mods[layer, :b].reshape(b, 1, 6, d)
        mc = jnp.broadcast_to(mods[layer, b].reshape(1, 1, 6, d), (b, 1, 6, d))
        sh1, sc1, g1, sh2, sc2, g2 = [mx[:, :, k] for k in range(6)]
        csh1, csc1, cg1, csh2, csc2, cg2 = [mc[:, :, k] for k in range(6)]

        w_in_p = jnp.pad(w_in[layer].astype(BF16), ((0, 0), (0, IN_COLS_PAD - IN_COLS)))
        wuq, wuk, wuv = _mla_weight_slots(mla_w_uq[layer], mla_w_ukv[layer])
        n1 = norm1_g[layer][None, :]
        qg = mla_q_norm_g[layer][None, :]
        kvg = mla_kv_norm_g[layer][None, :]
        px = _mixer_inputs(x, sh1, sc1, n1, w_in_p, qg, kvg, wuq, wuk, wuv, rope_x)
        pc = _mixer_inputs(ctx, csh1, csc1, n1, w_in_p, qg, kvg, wuq, wuk, wuv, rope_c)
        cb, u, sq, skp, svp, fu, mq, mk, mvp = px
        ccb, cu, csq, cskp, csvp, cfu, cmq, cmk, cmvp = pc

        sink = swa_sink[layer]
        og = out_norm_g[layer][None, :]
        wo = w_out[layer].astype(BF16)
        n2 = norm2_g[layer][None, :]
        wr = jnp.pad(w_router[layer], ((0, 0), (0, LANES - N_EXPERTS)))

        yx = (_short_conv(cb, u, conv_w[layer]), _swa(sink, sq, cskp, csvp, skp, svp),
              _fourier(fu), _mla(mq, cmk, cmvp, mk, mvp))
        x, fx, aff = _mixer_out(x, g1, *yx, og, wo, n2, sh2, sc2, wr)
        srcs = [fx.reshape(b * n, d)]
        seg_rows = [b * cap_x]
        wx, idx_x, bnd_x, rows_x = _route(aff, cap_x, tile_x, n)
        row_parts = [rows_x]
        if not last:
            yc = (_short_conv(ccb, cu, conv_w[layer]), _swa(sink, csq, cskp, csvp),
                  _fourier(cfu), _mla(cmq, cmk, cmvp))
            ctx, fc, affc = _mixer_out(ctx, cg1, *yc, og, wo, n2, csh2, csc2, wr)
            srcs.append(fc.reshape(b * l, d))
            seg_rows.append(b * cap_c)
            wc, idx_c, bnd_c, rows_c = _route(affc, cap_c, tile_c, l)
            row_parts.append(rows_c)
        idx_all = jnp.concatenate(row_parts, axis=1)[:, None, :]
        y = _expert_ffn(idx_all, w_gate[layer], w_up[layer], w_down[layer], srcs, seg_rows)
        rows_per_expert = y.shape[1]
        y_flat = y.reshape(N_EXPERTS * rows_per_expert, d)
        x = _combine(idx_x, bnd_x, x, g2, wx, y_flat, tile_x, 0, rows_per_expert,
                     final_g=final_norm_g[None, :] if last else None)
        if not last:
            ctx = _combine(idx_c, bnd_c, ctx, cg2, wc, y_flat, tile_c, b * cap_x, rows_per_expert)
    return x
```

```python
import functools
import math

import jax
import jax.numpy as jnp
from jax import lax
from jax.experimental import pallas as pl
from jax.experimental.pallas import tpu as pltpu

F32 = jnp.float32
BF16 = jnp.bfloat16
HIGHEST = lax.Precision.HIGHEST

GRID_W = 64
GROUP_W = 512
N_GROUPS = 4
SWA_HEAD_DIM = 64
SWA_HEADS = 8
SWA_KV_HEADS = 2
SWA_BLOCK = 128
MLA_HEADS = 8
MLA_NOPE = 64
MLA_ROPE = 32
MLA_V = 64
MLA_Q_LORA = 512
MLA_KV_LORA = 256
N_EXPERTS = 16
EC_CAPACITY = 2
ROPE_THETA = 10000.0
EPS = 1e-6
IN_COLS = 3616
IN_COLS_PAD = 3712
LANES = 128
SUBLANES = 8
DMA_ISSUE_UNROLL = 4
COMBINE_ROWS = 16
NEG = -1e30
LOG2E = 1.4426950408889634

VMEM_LIMIT = 60 * 1024 * 1024


def _cparams(sem):
    return pltpu.CompilerParams(dimension_semantics=sem, vmem_limit_bytes=VMEM_LIMIT)


def _dot(a, b):
    return jnp.dot(a, b, preferred_element_type=F32)


def _dot_t(a, b):
    return lax.dot_general(a, b, (((1,), (1,)), ((), ())), preferred_element_type=F32)


def _rms(x, g):
    return x * lax.rsqrt(jnp.mean(x * x, axis=-1, keepdims=True) + EPS) * g


def _adaln_kernel(c_ref, w_ref, b_ref, o_ref):
    c = c_ref[...]
    s = c / (1.0 + jnp.exp(-c))
    o_ref[0] = lax.dot_general(s, w_ref[0], (((1,), (0,)), ((), ())),
                               precision=HIGHEST, preferred_element_type=F32) + b_ref[0]


def _adaln(c_rows, ada_w, ada_b):
    depth, d, d6 = ada_w.shape
    tn = 1024 if d6 % 1024 == 0 else d6
    rows = c_rows.shape[0]
    return pl.pallas_call(
        _adaln_kernel,
        grid=(depth, d6 // tn),
        in_specs=[
            pl.BlockSpec((rows, d), lambda l, j: (0, 0)),
            pl.BlockSpec((1, d, tn), lambda l, j: (l, 0, j)),
            pl.BlockSpec((1, 1, tn), lambda l, j: (l, 0, j)),
        ],
        out_specs=pl.BlockSpec((1, rows, tn), lambda l, j: (l, 0, j)),
        out_shape=jax.ShapeDtypeStruct((depth, rows, d6), F32),
        compiler_params=_cparams(("arbitrary", "arbitrary")),
        name="adaln",
    )(c_rows, ada_w, ada_b.reshape(depth, 1, d6))


def _swap_halves(x, width):
    lane = lax.broadcasted_iota(jnp.int32, x.shape, 1)
    first = (lane % (2 * width)) < width
    return jnp.where(first, pltpu.roll(x, LANES - width, 1), pltpu.roll(x, width, 1))


def _mixin_kernel(x_ref, sh_ref, sc_ref, g_ref, w_ref, qg_ref, kvg_ref, wuq_ref, wuk_ref, wuv_ref,
                  css_ref, sns_ref, csm_ref, snm_ref,
                  cb_ref, u_ref, sq_ref, skp_ref, svp_ref, fu_ref, mq_ref, mk_ref, mvp_ref):
    x = x_ref[0]
    h = _rms(x, g_ref[...]) * (1.0 + sc_ref[0]) + sh_ref[0]
    hb = h.astype(BF16)

    pc = _dot(hb, w_ref[:, 0:3 * GROUP_W])
    cb_ref[0] = pc[:, 0:GROUP_W].astype(BF16)
    u_ref[0] = (pc[:, GROUP_W:2 * GROUP_W] * pc[:, 2 * GROUP_W:3 * GROUP_W]).astype(BF16)

    c0 = 3 * GROUP_W
    ps = _dot(hb, w_ref[:, c0:c0 + 768])
    css = css_ref[...]
    sns = sns_ref[...]
    qscale = SWA_HEAD_DIM ** -0.5 * LOG2E
    for c in range(4):
        t = ps[:, c * LANES:(c + 1) * LANES]
        t = (t * css + _swap_halves(t, SWA_HEAD_DIM // 2) * sns) * qscale
        sq_ref[0, :, c * LANES:(c + 1) * LANES] = t.astype(BF16)
    sk = ps[:, 512:640]
    sk = sk * css + _swap_halves(sk, SWA_HEAD_DIM // 2) * sns
    sv = ps[:, 640:768]
    lane = lax.broadcasted_iota(jnp.int32, sk.shape, 1)
    lo = lane < SWA_HEAD_DIM
    for src, dst, fill in ((sk, skp_ref, 0.0), (sv, svp_ref, 1.0)):
        sw = pltpu.roll(src, SWA_HEAD_DIM, 1)
        pad_hi = jnp.where(lane == SWA_HEAD_DIM, fill, 0.0)
        pad_lo = jnp.where(lane == 0, fill, 0.0)
        dst[0, :, 0:128] = jnp.where(lo, src, pad_hi).astype(BF16)
        dst[0, :, 128:256] = jnp.where(lo, pad_lo, sw).astype(BF16)
        dst[0, :, 256:384] = jnp.where(lo, sw, pad_hi).astype(BF16)
        dst[0, :, 384:512] = jnp.where(lo, pad_lo, src).astype(BF16)

    c0 += 768
    fu_ref[0] = _dot(hb, w_ref[:, c0:c0 + GROUP_W]).astype(BF16)

    c0 += GROUP_W
    pm = _dot(hb, w_ref[:, c0:c0 + 896])
    qn = _rms(pm[:, 0:MLA_Q_LORA], qg_ref[...]).astype(BF16)
    kvn = _rms(pm[:, MLA_Q_LORA:MLA_Q_LORA + MLA_KV_LORA], kvg_ref[...]).astype(BF16)
    csm = csm_ref[...]
    snm = snm_ref[...]

    def rope_m(t):
        lane_m = lax.broadcasted_iota(jnp.int32, t.shape, 1)
        sw = jnp.where(lane_m < MLA_NOPE + MLA_ROPE // 2,
                       pltpu.roll(t, LANES - MLA_ROPE // 2, 1), pltpu.roll(t, MLA_ROPE // 2, 1))
        return t * csm + sw * snm

    q = _dot(qn, wuq_ref[...])
    mscale = (MLA_NOPE + MLA_ROPE) ** -0.5 * LOG2E
    kpe = rope_m(pltpu.roll(pm[:, 768:896], MLA_NOPE, 1))
    kn = _dot(kvn, wuk_ref[...])
    mv = _dot(kvn, wuv_ref[...])
    lane_v = lax.broadcasted_iota(jnp.int32, kpe.shape, 1)
    for hd in range(MLA_HEADS):
        sl = slice(hd * LANES, (hd + 1) * LANES)
        mq_ref[0, :, sl] = (rope_m(q[:, sl]) * mscale).astype(BF16)
        mk_ref[0, :, sl] = (kn[:, sl] + kpe).astype(BF16)
        ones_lane = MLA_V if hd % 2 == 0 else 0
        mvp_ref[0, :, sl] = jnp.where(lane_v == ones_lane, 1.0, mv[:, sl]).astype(BF16)


def _mixer_inputs(x, shift, scale, norm_g, w_in_p, qg, kvg, wuq, wuk, wuv, rope_tabs):
    b, n, d = x.shape
    tm = min(256, n)
    css, sns, csm, snm = rope_tabs
    row = lambda bi, i: (bi, i, 0)
    const2 = lambda bi, i: (0, 0)
    tab = lambda bi, i: (i, 0)
    outs = [jax.ShapeDtypeStruct((b, n, w), BF16) for w in (512, 512, 512, 512, 512, 512, 1024, 1024, 1024)]
    return pl.pallas_call(
        _mixin_kernel,
        grid=(b, n // tm),
        in_specs=[
            pl.BlockSpec((1, tm, d), row),
            pl.BlockSpec((1, 1, d), lambda bi, i: (bi, 0, 0)),
            pl.BlockSpec((1, 1, d), lambda bi, i: (bi, 0, 0)),
            pl.BlockSpec((1, d), const2),
            pl.BlockSpec((d, IN_COLS_PAD), const2),
            pl.BlockSpec((1, MLA_Q_LORA), const2),
            pl.BlockSpec((1, MLA_KV_LORA), const2),
            pl.BlockSpec((MLA_Q_LORA, 1024), const2),
            pl.BlockSpec((MLA_KV_LORA, 1024), const2),
            pl.BlockSpec((MLA_KV_LORA, 1024), const2),
            pl.BlockSpec((tm, LANES), tab),
            pl.BlockSpec((tm, LANES), tab),
            pl.BlockSpec((tm, LANES), tab),
            pl.BlockSpec((tm, LANES), tab),
        ],
        out_specs=[pl.BlockSpec((1, tm, o.shape[-1]), row) for o in outs],
        out_shape=outs,
        compiler_params=_cparams(("arbitrary", "arbitrary")),
        name="mixer_inputs",
    )(x, shift, scale, norm_g, w_in_p, qg, kvg, wuq, wuk, wuv, css, sns, csm, snm)


def _swa_kernel(sink_ref, q_ref, kc_ref, vc_ref, *rest, has_band, n_blocks):
    if has_band:
        kp_ref, vp_ref, o_ref = rest
    else:
        (o_ref,) = rest
    i = pl.program_id(1)
    t = SWA_BLOCK
    if has_band:
        start = pl.multiple_of(jnp.clip(i - 1, 0, n_blocks - 3) * t, t)
        qpos = i * t + lax.broadcasted_iota(jnp.int32, (2 * t, 3 * t), 0) % t
        kpos = start + lax.broadcasted_iota(jnp.int32, (2 * t, 3 * t), 1)
        valid = jnp.abs(kpos - qpos) <= SWA_BLOCK
    first = lax.broadcasted_iota(jnp.int32, (2 * t, 1), 0) < t
    lane = lax.broadcasted_iota(jnp.int32, (2 * t, LANES), 1)
    acc = [None] * 4
    for g in range(SWA_KV_HEADS):
        for hh in range(2):
            slot = 2 * g + hh
            sl = slice(slot * LANES, (slot + 1) * LANES)
            cols = (2 * g, 2 * g + 1)
            q2 = jnp.concatenate([q_ref[0, :, c * LANES:(c + 1) * LANES] for c in cols], axis=0)
            sink = jnp.where(first, sink_ref[2 * cols[0] + hh], sink_ref[2 * cols[1] + hh]) * LOG2E
            s_ctx = _dot_t(q2, kc_ref[0, :, sl])
            m = jnp.maximum(jnp.max(s_ctx, axis=-1, keepdims=True), sink)
            if has_band:
                s_b = jnp.where(valid, _dot_t(q2, kp_ref[0, pl.ds(start, 3 * t), sl]), NEG)
                m = jnp.maximum(m, jnp.max(s_b, axis=-1, keepdims=True))
            o = _dot(jnp.exp2(s_ctx - m).astype(BF16), vc_ref[0, :, sl])
            if has_band:
                o = o + _dot(jnp.exp2(s_b - m).astype(BF16), vp_ref[0, pl.ds(start, 3 * t), sl])
            ones_lane = SWA_HEAD_DIM if hh == 0 else 0
            den = o[:, ones_lane:ones_lane + 1] + jnp.exp2(sink - m)
            keep = (lane < SWA_HEAD_DIM) if hh == 0 else (lane >= SWA_HEAD_DIM)
            o = jnp.where(keep, o / den, 0.0)
            for k, c in enumerate(cols):
                part = o[k * t:(k + 1) * t]
                acc[c] = part if acc[c] is None else acc[c] + part
    for c in range(4):
        o_ref[0, :, c * LANES:(c + 1) * LANES] = acc[c].astype(BF16)


def _swa(sink, q, kc, vc, kp=None, vp=None):
    b, n, _ = q.shape
    l = kc.shape[1]
    t = SWA_BLOCK
    has_band = kp is not None
    blk = lambda bi, i: (bi, i, 0)
    whole = lambda bi, i: (bi, 0, 0)
    in_specs = [
        pl.BlockSpec(memory_space=pltpu.SMEM),
        pl.BlockSpec((1, t, 512), blk),
        pl.BlockSpec((1, l, 512), whole),
        pl.BlockSpec((1, l, 512), whole),
    ]
    args = [sink, q, kc, vc]
    if has_band:
        in_specs += [pl.BlockSpec((1, n, 512), whole), pl.BlockSpec((1, n, 512), whole)]
        args += [kp, vp]
    return pl.pallas_call(
        functools.partial(_swa_kernel, has_band=has_band, n_blocks=n // t),
        grid=(b, n // t),
        in_specs=in_specs,
        out_specs=pl.BlockSpec((1, t, 512), blk),
        out_shape=jax.ShapeDtypeStruct((b, n, 512), BF16),
        compiler_params=_cparams(("arbitrary", "arbitrary")),
        name="swa_latent" if has_band else "swa_context",
    )(*args)


def _mla_kernel(q_ref, kc_ref, vc_ref, *rest, has_lat):
    if has_lat:
        k_ref, v_ref, o_ref = rest
    else:
        (o_ref,) = rest
    lane = lax.broadcasted_iota(jnp.int32, (q_ref.shape[1], LANES), 1)
    out = None
    for hh in range(2):
        sl = slice(hh * LANES, (hh + 1) * LANES)
        q = q_ref[0, :, sl]
        s_ctx = _dot_t(q, kc_ref[0, :, sl])
        m = jnp.max(s_ctx, axis=-1, keepdims=True)
        if has_lat:
            s_lat = _dot_t(q, k_ref[0, :, sl])
            m = jnp.maximum(m, jnp.max(s_lat, axis=-1, keepdims=True))
        o = _dot(jnp.exp2(s_ctx - m).astype(BF16), vc_ref[0, :, sl])
        if has_lat:
            o = o + _dot(jnp.exp2(s_lat - m).astype(BF16), v_ref[0, :, sl])
        ones_lane = MLA_V if hh == 0 else 0
        keep = (lane < MLA_V) if hh == 0 else (lane >= MLA_V)
        o = jnp.where(keep, o / o[:, ones_lane:ones_lane + 1], 0.0)
        out = o if out is None else out + o
    o_ref[0] = out.astype(BF16)


def _mla(q, kc, vc, k=None, v=None):
    b, n, _ = q.shape
    l = kc.shape[1]
    tq = min(256, n)
    has_lat = k is not None
    pair = 2 * LANES
    in_specs = [
        pl.BlockSpec((1, tq, pair), lambda bi, h, i: (bi, i, h)),
        pl.BlockSpec((1, l, pair), lambda bi, h, i: (bi, 0, h)),
        pl.BlockSpec((1, l, pair), lambda bi, h, i: (bi, 0, h)),
    ]
    args = [q, kc, vc]
    if has_lat:
        in_specs += [pl.BlockSpec((1, n, pair), lambda bi, h, i: (bi, 0, h)),
                     pl.BlockSpec((1, n, pair), lambda bi, h, i: (bi, 0, h))]
        args += [k, v]
    return pl.pallas_call(
        functools.partial(_mla_kernel, has_lat=has_lat),
        grid=(b, MLA_HEADS // 2, n // tq),
        in_specs=in_specs,
        out_specs=pl.BlockSpec((1, tq, LANES), lambda bi, h, i: (bi, i, h)),
        out_shape=jax.ShapeDtypeStruct((b, n, MLA_HEADS // 2 * LANES), BF16),
        compiler_params=_cparams(("arbitrary", "arbitrary", "arbitrary")),
        name="mla_latent" if has_lat else "mla_context",
    )(*args)


def _mixout_kernel(x_ref, g1_ref, yc_ref, ys_ref, yf_ref, ym_ref, og_ref, wo_ref,
                   n2_ref, sh2_ref, sc2_ref, wr_ref, xo_ref, fx_ref, aff_ref):
    groups = (yc_ref[0].astype(F32), ys_ref[0].astype(F32), yf_ref[0].astype(F32), ym_ref[0].astype(F32))
    acc = None
    for gi, y in enumerate(groups):
        sl = slice(gi * GROUP_W, (gi + 1) * GROUP_W)
        yn = _rms(y, og_ref[:, sl]).astype(BF16)
        part = _dot(yn, wo_ref[sl, :])
        acc = part if acc is None else acc + part
    xn = x_ref[0] + g1_ref[0] * acc
    xo_ref[0] = xn
    fx = _rms(xn, n2_ref[...]) * (1.0 + sc2_ref[0]) + sh2_ref[0]
    fx_ref[0] = fx
    fh = fx.astype(BF16)
    fl = (fx - fh.astype(F32)).astype(BF16)
    parts = _dot(fh, wr_ref[...]) + _dot(fl, wr_ref[...])
    logits = parts + pltpu.roll(parts, LANES - N_EXPERTS, 1)
    lane = lax.broadcasted_iota(jnp.int32, logits.shape, 1)
    logits = jnp.where(lane < N_EXPERTS, logits, NEG)
    e = jnp.exp(logits - jnp.max(logits, axis=-1, keepdims=True))
    aff_ref[0] = e / jnp.sum(e, axis=-1, keepdims=True)


def _mixer_out(x, g1, yc, ys, yf, ym, out_norm_g, w_out_b, norm2_g, sh2, sc2, w_router_p):
    b, n, d = x.shape
    tm = min(256, n)
    row = lambda bi, i: (bi, i, 0)
    per_b = lambda bi, i: (bi, 0, 0)
    const2 = lambda bi, i: (0, 0)
    return pl.pallas_call(
        _mixout_kernel,
        grid=(b, n // tm),
        in_specs=[
            pl.BlockSpec((1, tm, d), row),
            pl.BlockSpec((1, 1, d), per_b),
            pl.BlockSpec((1, tm, GROUP_W), row),
            pl.BlockSpec((1, tm, GROUP_W), row),
            pl.BlockSpec((1, tm, GROUP_W), row),
            pl.BlockSpec((1, tm, GROUP_W), row),
            pl.BlockSpec((1, N_GROUPS * GROUP_W), const2),
            pl.BlockSpec((N_GROUPS * GROUP_W, d), const2),
            pl.BlockSpec((1, d), const2),
            pl.BlockSpec((1, 1, d), per_b),
            pl.BlockSpec((1, 1, d), per_b),
            pl.BlockSpec((d, LANES), const2),
        ],
        out_specs=[pl.BlockSpec((1, tm, d), row), pl.BlockSpec((1, tm, d), row),
                   pl.BlockSpec((1, tm, LANES), row)],
        out_shape=[jax.ShapeDtypeStruct((b, n, d), F32), jax.ShapeDtypeStruct((b, n, d), F32),
                   jax.ShapeDtypeStruct((b, n, LANES), F32)],
        compiler_params=_cparams(("arbitrary", "arbitrary")),
        name="mixer_out",
    )(x, g1, yc, ys, yf, ym, out_norm_g, w_out_b, norm2_g, sh2, sc2, w_router_p)


def _ffn_kernel(idx_ref, wg_ref, wu_ref, wd_ref, *rest, seg_rows, chunk, n_f, tf):
    n_src = len(seg_rows)
    srcs = rest[:n_src]
    y_ref = rest[n_src]
    stage_ref, xs_ref, act_ref, sem = rest[n_src + 1:]
    s = pl.program_id(1)

    @pl.when(s == 0)
    def _gather():
        base = 0
        for src, rows in zip(srcs, seg_rows):
            for c0 in range(0, rows, chunk):
                cn = min(chunk, rows - c0)
                off = base + c0

                def issue(j, carry, off=off, src=src):
                    r = idx_ref[0, 0, off + j]
                    pltpu.make_async_copy(src.at[pl.ds(r, 1), :], stage_ref.at[pl.ds(j, 1), :], sem).start()
                    return carry

                lax.fori_loop(0, cn, issue, 0, unroll=DMA_ISSUE_UNROLL)
                pltpu.make_async_copy(src.at[pl.ds(0, cn), :], stage_ref.at[pl.ds(0, cn), :], sem).wait()
                xs_ref[off:off + cn, :] = stage_ref[0:cn, :].astype(BF16)
            base += rows

    @pl.when(s < n_f)
    def _gate_up():
        xs = xs_ref[...]
        a = _dot(xs, wg_ref[0, 0].astype(BF16))
        u = _dot(xs, wu_ref[0, 0].astype(BF16))
        act_ref[s] = (a / (1.0 + jnp.exp(-a)) * u).astype(BF16)

    @pl.when(s >= n_f)
    def _down():
        acc = None
        for k in range(n_f):
            part = _dot(act_ref[k], wd_ref[0, 0, k * tf:(k + 1) * tf, :].astype(BF16))
            acc = part if acc is None else acc + part
        y_ref[0] = acc


def _expert_ffn(idx_all, layer, w_gate, w_up, w_down, srcs, seg_rows):
    _, e, d, ff = w_gate.shape
    r = idx_all.shape[-1]
    tf = min(256, ff)
    td = min(256, d)
    n_f = ff // tf
    n_d = d // td
    chunk = min(512, max(seg_rows))
    kern = functools.partial(_ffn_kernel, seg_rows=tuple(seg_rows), chunk=chunk, n_f=n_f, tf=tf)
    return pl.pallas_call(
        kern,
        grid=(e, n_f + n_d),
        in_specs=[
            pl.BlockSpec((1, 1, r), lambda ei, s: (ei, 0, 0), memory_space=pltpu.SMEM),
            pl.BlockSpec((1, 1, d, tf), lambda ei, s: (layer, ei, 0, jnp.minimum(s, n_f - 1))),
            pl.BlockSpec((1, 1, d, tf), lambda ei, s: (layer, ei, 0, jnp.minimum(s, n_f - 1))),
            pl.BlockSpec((1, 1, ff, td), lambda ei, s: (layer, ei, 0, jnp.maximum(s - n_f, 0))),
        ] + [pl.BlockSpec(memory_space=pl.ANY) for _ in srcs],
        out_specs=pl.BlockSpec((1, r, td), lambda ei, s: (ei, 0, jnp.maximum(s - n_f, 0))),
        out_shape=jax.ShapeDtypeStruct((e, r, d), F32),
        scratch_shapes=[
            pltpu.VMEM((chunk, d), F32),
            pltpu.VMEM((r, d), BF16),
            pltpu.VMEM((n_f, r, tf), BF16),
            pltpu.SemaphoreType.DMA(()),
        ],
        compiler_params=_cparams(("arbitrary", "arbitrary")),
        name="expert_ffn",
    )(idx_all, w_gate, w_up, w_down, *srcs)


def _select_kernel(aff_ref, w_ref, idx_ref, bnd_ref, pos_ref, post_ref, *, cap, tile, jchunk):
    n = aff_ref.shape[1]
    bits = pltpu.bitcast(aff_ref[0], jnp.int32)

    def search(it, thr):
        cand = thr | jnp.left_shift(jnp.int32(1), 30 - it)
        cnt = jnp.sum(jnp.where(bits >= cand, 1, 0), axis=0, keepdims=True)
        return jnp.where(cnt >= cap, cand, thr)

    thr = lax.fori_loop(0, 31, search, jnp.zeros((1, LANES), jnp.int32))
    need = (cap - jnp.sum(jnp.where(bits > thr, 1, 0), axis=0, keepdims=True)).astype(F32)

    r = lax.broadcasted_iota(jnp.int32, (tile, tile), 0)
    c = lax.broadcasted_iota(jnp.int32, (tile, tile), 1)
    tri = jnp.where(c <= r, 1.0, 0.0).astype(BF16)

    def chunk(k, carry):
        ceq, cpos = carry
        r0 = pl.multiple_of(k * tile, tile)
        a = aff_ref[0, pl.ds(r0, tile), :]
        bt = pltpu.bitcast(a, jnp.int32)
        eq = bt == thr
        rank = _dot(tri, jnp.where(eq, 1.0, 0.0).astype(BF16)) + ceq
        sel = jnp.where((bt > thr) | (eq & (rank <= need)), 1.0, 0.0)
        pos = _dot(tri, sel.astype(BF16)) + cpos
        pos_ref[pl.ds(r0, tile), :] = pos
        w_ref[0, pl.ds(r0, tile), :] = sel * a
        cpos = pos[tile - 1:tile, :]
        bnd_ref[0, pl.ds(k, 1), :] = cpos.astype(jnp.int32)
        return rank[tile - 1:tile, :], cpos

    zero = jnp.zeros((1, LANES), F32)
    lax.fori_loop(0, n // tile, chunk, (zero, zero))

    for k in range(n // tile):
        post_ref[:, k * tile:(k + 1) * tile] = pos_ref[k * tile:(k + 1) * tile, :].T

    def per_expert(e, carry):
        row = post_ref[pl.ds(e, 1), :]

        def per_j(jc, carry_j):
            j0 = pl.multiple_of(jc * jchunk, jchunk)
            jv = (lax.broadcasted_iota(jnp.int32, (jchunk, n), 0) + j0).astype(F32)
            cnt = jnp.sum(jnp.where(row <= jv, 1.0, 0.0), axis=-1, keepdims=True)
            idx_ref[0, e, pl.ds(j0, jchunk), :] = cnt.astype(jnp.int32)
            return carry_j

        return lax.fori_loop(0, cap // jchunk, per_j, carry)

    lax.fori_loop(0, N_EXPERTS, per_expert, 0)


def _select(aff, cap, tile):
    b, n, _ = aff.shape
    jchunk = min(64, cap)
    kern = functools.partial(_select_kernel, cap=cap, tile=tile, jchunk=jchunk)
    return pl.pallas_call(
        kern,
        grid=(b,),
        in_specs=[pl.BlockSpec((1, n, LANES), lambda bi: (bi, 0, 0))],
        out_specs=[pl.BlockSpec((1, n, LANES), lambda bi: (bi, 0, 0)),
                   pl.BlockSpec((1, N_EXPERTS, cap, 1), lambda bi: (bi, 0, 0, 0)),
                   pl.BlockSpec((1, n // tile, LANES), lambda bi: (bi, 0, 0))],
        out_shape=[jax.ShapeDtypeStruct((b, n, LANES), F32),
                   jax.ShapeDtypeStruct((b, N_EXPERTS, cap, 1), jnp.int32),
                   jax.ShapeDtypeStruct((b, n // tile, LANES), jnp.int32)],
        scratch_shapes=[pltpu.VMEM((n, LANES), F32), pltpu.VMEM((LANES, n), F32)],
        compiler_params=_cparams(("arbitrary",)),
        name="select",
    )(aff)


def _combine_kernel(idx_ref, bnd_ref, x_ref, g2_ref, w_ref, *rest, tile, cap, n_tiles, row0, rows_per_expert,
                    final_norm):
    if final_norm:
        fg_ref, y_hbm, o_ref, buf, cnt_ref, sem = rest
    else:
        y_hbm, o_ref, buf, cnt_ref, sem = rest
    b = pl.program_id(0)
    i = pl.program_id(1)
    plane_rows = N_EXPERTS * tile
    unroll = DMA_ISSUE_UNROLL

    @pl.when((b == 0) & (i == 0))
    def _init():
        buf[...] = jnp.zeros(buf.shape, F32)

    @pl.when(i < n_tiles)
    def _issue():
        slot = i % 2
        dst = buf.at[slot]
        dsem = sem.at[slot]
        t0 = i * tile
        total = jnp.int32(0)
        for e in range(N_EXPERTS):
            lo = bnd_ref[0, i, e]
            hi = bnd_ref[0, i + 1, e]
            src0 = e * rows_per_expert + row0 + b * cap
            groups = lax.shift_right_logical(hi - lo + (unroll - 1), unroll.bit_length() - 1)

            def group(k, carry, e=e, lo=lo, hi=hi, src0=src0):
                for u in range(unroll):
                    j = lo + k * unroll + u
                    ok = j < hi
                    jj = jnp.minimum(j, cap - 1)
                    drow = jnp.where(ok, e * tile + idx_ref[0, e, jj] - t0, plane_rows + unroll * e + u)
                    srow = jnp.where(ok, src0 + jj, 0)
                    pltpu.make_async_copy(y_hbm.at[pl.ds(srow, 1), :], dst.at[pl.ds(drow, 1), :], dsem).start()
                return carry

            lax.fori_loop(0, groups, group, 0)
            total = total + groups * unroll

        pad = total & (SUBLANES - 1)

        def issue_pad(k, carry):
            pltpu.make_async_copy(y_hbm.at[pl.ds(k, 1), :],
                                  dst.at[pl.ds(plane_rows + unroll * N_EXPERTS + k, 1), :], dsem).start()
            return carry

        lax.fori_loop(0, pad, issue_pad, 0)
        cnt_ref[slot] = total + pad

    @pl.when(i > 0)
    def _reduce():
        slot = (i - 1) % 2
        n_rows = pl.multiple_of(cnt_ref[slot], SUBLANES)

        @pl.when(n_rows > 0)
        def _wait():
            pltpu.make_async_copy(y_hbm.at[pl.ds(0, n_rows), :], buf.at[slot, pl.ds(0, n_rows), :],
                                  sem.at[slot]).wait()

        def rows(g, carry):
            r0 = pl.multiple_of(g * COMBINE_ROWS, COMBINE_ROWS)
            wv = w_ref[0, pl.ds(r0, COMBINE_ROWS), :]
            acc = None
            for e in range(N_EXPERTS):
                part = buf[slot, pl.ds(e * tile + r0, COMBINE_ROWS), :] * wv[:, e:e + 1]
                acc = part if acc is None else acc + part
            out = x_ref[0, pl.ds(r0, COMBINE_ROWS), :] + g2_ref[0] * acc
            if final_norm:
                out = _rms(out, fg_ref[...])
            o_ref[0, pl.ds(r0, COMBINE_ROWS), :] = out
            return carry

        lax.fori_loop(0, tile // COMBINE_ROWS, rows, 0)


def _combine(idx, bnd, x, g2, w, y_flat, tile, row0, rows_per_expert, final_g=None):
    b, n, d = x.shape
    cap = idx.shape[-1]
    n_tiles = n // tile
    final_norm = final_g is not None
    kern = functools.partial(_combine_kernel, tile=tile, cap=cap, n_tiles=n_tiles, row0=row0,
                             rows_per_expert=rows_per_expert, final_norm=final_norm)
    row = lambda bi, i: (bi, jnp.maximum(i - 1, 0), 0)
    per_b = lambda bi, i: (bi, 0, 0)
    spare = DMA_ISSUE_UNROLL * N_EXPERTS + SUBLANES
    in_specs = [
        pl.BlockSpec((1, N_EXPERTS, cap), per_b, memory_space=pltpu.SMEM),
        pl.BlockSpec((1, n // tile + 1, N_EXPERTS), per_b, memory_space=pltpu.SMEM),
        pl.BlockSpec((1, tile, d), row),
        pl.BlockSpec((1, 1, d), per_b),
        pl.BlockSpec((1, tile, LANES), row),
    ]
    args = [idx, bnd, x, g2, w]
    if final_norm:
        in_specs.append(pl.BlockSpec((1, d), lambda bi, i: (0, 0)))
        args.append(final_g)
    in_specs.append(pl.BlockSpec(memory_space=pl.ANY))
    args.append(y_flat)
    return pl.pallas_call(
        kern,
        grid=(b, n_tiles + 1),
        in_specs=in_specs,
        out_specs=pl.BlockSpec((1, tile, d), row),
        out_shape=jax.ShapeDtypeStruct((b, n, d), F32),
        scratch_shapes=[pltpu.VMEM((2, N_EXPERTS * tile + spare, d), F32), pltpu.SMEM((2,), jnp.int32),
                        pltpu.SemaphoreType.DMA((2,))],
        compiler_params=_cparams(("arbitrary", "arbitrary")),
        name="combine",
    )(*args)


def _route(aff, cap, tile, row_stride):
    b = aff.shape[0]
    w, idx, bnd = _select(aff, cap, tile)
    idx = idx.reshape(b, N_EXPERTS, cap)
    bnd = jnp.pad(bnd[:, :, :N_EXPERTS], ((0, 0), (1, 0), (0, 0)))
    rows = idx + (jnp.arange(b, dtype=jnp.int32) * row_stride)[:, None, None]
    rows = jnp.swapaxes(rows, 0, 1).reshape(N_EXPERTS, b * cap)
    return w, idx, bnd, rows


def _axial_tables(n_tok, rot_dim):
    rows = (jnp.arange(n_tok, dtype=jnp.int32) // GRID_W).astype(F32)
    cols = (jnp.arange(n_tok, dtype=jnp.int32) % GRID_W).astype(F32)
    n_freq = rot_dim // 4
    inv_freq = ROPE_THETA ** (-jnp.arange(n_freq, dtype=F32) / n_freq)
    ang = jnp.concatenate([rows[:, None] * inv_freq, cols[:, None] * inv_freq], axis=-1)
    return jnp.cos(ang), jnp.sin(ang)


def _rope_tables(n_tok, identity):
    if identity:
        one = jnp.ones((n_tok, LANES), F32)
        zero = jnp.zeros((n_tok, LANES), F32)
        return one, zero, one, zero
    cos_s, sin_s = _axial_tables(n_tok, SWA_HEAD_DIM)
    css = jnp.tile(jnp.concatenate([cos_s, cos_s], axis=-1), (1, 2))
    sns = jnp.tile(jnp.concatenate([-sin_s, sin_s], axis=-1), (1, 2))
    cos_m, sin_m = _axial_tables(n_tok, MLA_ROPE)
    ones = jnp.ones((n_tok, MLA_NOPE), F32)
    tail = LANES - MLA_NOPE - MLA_ROPE
    csm = jnp.concatenate([ones, cos_m, cos_m, jnp.ones((n_tok, tail), F32)], axis=-1)
    snm = jnp.concatenate([0.0 * ones, -sin_m, sin_m, jnp.zeros((n_tok, tail), F32)], axis=-1)
    return css, sns, csm, snm


def _mla_weight_slots(w_uq, w_ukv):
    dq = MLA_NOPE + MLA_ROPE
    wuq = jnp.pad(w_uq.reshape(MLA_Q_LORA, MLA_HEADS, dq), ((0, 0), (0, 0), (0, LANES - dq)))
    wkv = w_ukv.reshape(MLA_KV_LORA, MLA_HEADS, MLA_NOPE + MLA_V)
    wuk = jnp.pad(wkv[..., :MLA_NOPE], ((0, 0), (0, 0), (0, LANES - MLA_NOPE)))
    wv = wkv[..., MLA_NOPE:]
    even = jnp.pad(wv, ((0, 0), (0, 0), (0, LANES - MLA_V)))
    odd = jnp.pad(wv, ((0, 0), (0, 0), (LANES - MLA_V, 0)))
    is_even = (jnp.arange(MLA_HEADS) % 2 == 0)[None, :, None]
    wuv = jnp.where(is_even, even, odd)
    flat = lambda w: w.reshape(w.shape[0], MLA_HEADS * LANES).astype(BF16)
    return flat(wuq), flat(wuk), flat(wuv)


def _short_conv(cb, u, w):
    up = jnp.pad(u.astype(F32), ((0, 0), (1, 1), (0, 0)))
    y = up[:, :-2] * w[0] + up[:, 1:-1] * w[1] + up[:, 2:] * w[2]
    return (cb.astype(F32) * y).astype(BF16)


def _fourier(u):
    b, n, _ = u.shape
    uh = u.reshape(b, n, GROUP_W // 64, 64).astype(F32)
    y = jnp.fft.fft2(uh, axes=(1, 3), norm="ortho").real
    return y.reshape(b, n, GROUP_W).astype(F32)


def kernel(x, c, ctx, c_ctx, ada_w, ada_b, norm1_g, norm2_g, w_in, conv_w, swa_sink, mla_q_norm_g, mla_w_uq,
           mla_kv_norm_g, mla_w_ukv, out_norm_g, w_out, w_router, w_gate, w_up, w_down, final_norm_g):
    b, n, d = x.shape
    l = ctx.shape[1]
    depth = ada_w.shape[0]
    cap_x = EC_CAPACITY * n // N_EXPERTS
    cap_c = EC_CAPACITY * l // N_EXPERTS
    tile_x = min(128, n)
    tile_c = min(128, l)

    c_rows = jnp.concatenate([c, c_ctx[None, :], jnp.zeros((-(b + 1) % 8, d), F32)], axis=0)
    mods = _adaln(c_rows, ada_w, ada_b)
    rope_x = _rope_tables(n, identity=False)
    rope_c = _rope_tables(l, identity=True)

    for layer in range(depth):
        last = layer == depth - 1
        mx = mods[layer, :b].reshape(b, 1, 6, d)
        mc = jnp.broadcast_to(mods[layer, b].reshape(1, 1, 6, d), (b, 1, 6, d))
        sh1, sc1, g1, sh2, sc2, g2 = [mx[:, :, k] for k in range(6)]
        csh1, csc1, cg1, csh2, csc2, cg2 = [mc[:, :, k] for k in range(6)]

        w_in_p = jnp.pad(w_in[layer].astype(BF16), ((0, 0), (0, IN_COLS_PAD - IN_COLS)))
        wuq, wuk, wuv = _mla_weight_slots(mla_w_uq[layer], mla_w_ukv[layer])
        n1 = norm1_g[layer][None, :]
        qg = mla_q_norm_g[layer][None, :]
        kvg = mla_kv_norm_g[layer][None, :]
        px = _mixer_inputs(x, sh1, sc1, n1, w_in_p, qg, kvg, wuq, wuk, wuv, rope_x)
        pc = _mixer_inputs(ctx, csh1, csc1, n1, w_in_p, qg, kvg, wuq, wuk, wuv, rope_c)
        cb, u, sq, skp, svp, fu, mq, mk, mvp = px
        ccb, cu, csq, cskp, csvp, cfu, cmq, cmk, cmvp = pc

        sink = swa_sink[layer]
        og = out_norm_g[layer][None, :]
        wo = w_out[layer].astype(BF16)
        n2 = norm2_g[layer][None, :]
        wr_hi = w_router[layer].astype(BF16)
        wr_lo = (w_router[layer] - wr_hi.astype(F32)).astype(BF16)
        wr = jnp.pad(jnp.concatenate([wr_hi, wr_lo], axis=1), ((0, 0), (0, LANES - 2 * N_EXPERTS)))

        yx = (_short_conv(cb, u, conv_w[layer]), _swa(sink, sq, cskp, csvp, skp, svp),
              _fourier(fu), _mla(mq, cmk, cmvp, mk, mvp))
        x, fx, aff = _mixer_out(x, g1, *yx, og, wo, n2, sh2, sc2, wr)
        srcs = [fx.reshape(b * n, d)]
        seg_rows = [b * cap_x]
        wx, idx_x, bnd_x, rows_x = _route(aff, cap_x, tile_x, n)
        row_parts = [rows_x]
        if not last:
            yc = (_short_conv(ccb, cu, conv_w[layer]), _swa(sink, csq, cskp, csvp),
                  _fourier(cfu), _mla(cmq, cmk, cmvp))
            ctx, fc, affc = _mixer_out(ctx, cg1, *yc, og, wo, n2, csh2, csc2, wr)
            srcs.append(fc.reshape(b * l, d))
            seg_rows.append(b * cap_c)
            wc, idx_c, bnd_c, rows_c = _route(affc, cap_c, tile_c, l)
            row_parts.append(rows_c)
        idx_all = jnp.concatenate(row_parts, axis=1)[:, None, :]
        y = _expert_ffn(idx_all, layer, w_gate, w_up, w_down, srcs, seg_rows)
        rows_per_expert = y.shape[1]
        y_flat = y.reshape(N_EXPERTS * rows_per_expert, d)
        x = _combine(idx_x, bnd_x, x, g2, wx, y_flat, tile_x, 0, rows_per_expert,
                     final_g=final_norm_g[None, :] if last else None)
        if not last:
            ctx = _combine(idx_c, bnd_c, ctx, cg2, wc, y_flat, tile_c, b * cap_x, rows_per_expert)
    return x
```

```python
import functools
import math

import jax
import jax.numpy as jnp
from jax import lax
from jax.experimental import pallas as pl
from jax.experimental.pallas import tpu as pltpu

F32 = jnp.float32
BF16 = jnp.bfloat16
HIGHEST = lax.Precision.HIGHEST

GRID_W = 64
GROUP_W = 512
N_GROUPS = 4
SWA_HEAD_DIM = 64
SWA_HEADS = 8
SWA_KV_HEADS = 2
SWA_BLOCK = 128
FNET_HEAD_DIM = 64
MLA_HEADS = 8
MLA_NOPE = 64
MLA_ROPE = 32
MLA_V = 64
MLA_Q_LORA = 512
MLA_KV_LORA = 256
N_EXPERTS = 16
EC_CAPACITY = 2
ROPE_THETA = 10000.0
EPS = 1e-6
IN_COLS = 3616
IN_COLS_PAD = 3712
LANES = 128
SUBLANES = 8
DMA_ISSUE_UNROLL = 4
COMBINE_ROWS = 16
CONV_HALO = 16
NEG = -1e30
LOG2E = 1.4426950408889634

VMEM_LIMIT = 60 * 1024 * 1024


def _cparams(sem):
    return pltpu.CompilerParams(dimension_semantics=sem, vmem_limit_bytes=VMEM_LIMIT)


def _dot(a, b):
    return jnp.dot(a, b, preferred_element_type=F32)


def _dot_t(a, b):
    return lax.dot_general(a, b, (((1,), (1,)), ((), ())), preferred_element_type=F32)


def _rms(x, g):
    return x * lax.rsqrt(jnp.mean(x * x, axis=-1, keepdims=True) + EPS) * g


def _adaln_kernel(c_ref, w_ref, b_ref, o_ref):
    c = c_ref[...]
    s = c / (1.0 + jnp.exp(-c))
    o_ref[0] = lax.dot_general(s, w_ref[0], (((1,), (0,)), ((), ())),
                               precision=HIGHEST, preferred_element_type=F32) + b_ref[0]


def _adaln(c_rows, ada_w, ada_b):
    depth, d, d6 = ada_w.shape
    tn = 1024 if d6 % 1024 == 0 else d6
    rows = c_rows.shape[0]
    return pl.pallas_call(
        _adaln_kernel,
        grid=(depth, d6 // tn),
        in_specs=[
            pl.BlockSpec((rows, d), lambda l, j: (0, 0)),
            pl.BlockSpec((1, d, tn), lambda l, j: (l, 0, j)),
            pl.BlockSpec((1, 1, tn), lambda l, j: (l, 0, j)),
        ],
        out_specs=pl.BlockSpec((1, rows, tn), lambda l, j: (l, 0, j)),
        out_shape=jax.ShapeDtypeStruct((depth, rows, d6), F32),
        compiler_params=_cparams(("arbitrary", "arbitrary")),
        name="adaln",
    )(c_rows, ada_w, ada_b.reshape(depth, 1, d6))


def _swap_halves(x, width):
    lane = lax.broadcasted_iota(jnp.int32, x.shape, 1)
    first = (lane % (2 * width)) < width
    return jnp.where(first, pltpu.roll(x, LANES - width, 1), pltpu.roll(x, width, 1))


def _mixin_kernel(x_ref, sh_ref, sc_ref, g_ref, w_ref, qg_ref, kvg_ref, wuq_ref, wuk_ref, wuv_ref,
                  css_ref, sns_ref, csm_ref, snm_ref, cdft_ref,
                  cb_ref, u_ref, sq_ref, skp_ref, svp_ref, fu_ref, mq_ref, mk_ref, mvp_ref):
    x = x_ref[0]
    h = _rms(x, g_ref[...]) * (1.0 + sc_ref[0]) + sh_ref[0]
    hb = h.astype(BF16)

    pc = _dot(hb, w_ref[:, 0:3 * GROUP_W])
    cb_ref[0] = pc[:, 0:GROUP_W].astype(BF16)
    u_ref[0] = (pc[:, GROUP_W:2 * GROUP_W] * pc[:, 2 * GROUP_W:3 * GROUP_W]).astype(BF16)

    c0 = 3 * GROUP_W
    ps = _dot(hb, w_ref[:, c0:c0 + 768])
    css = css_ref[...]
    sns = sns_ref[...]
    qscale = SWA_HEAD_DIM ** -0.5 * LOG2E
    for c in range(4):
        t = ps[:, c * LANES:(c + 1) * LANES]
        t = (t * css + _swap_halves(t, SWA_HEAD_DIM // 2) * sns) * qscale
        sq_ref[0, :, c * LANES:(c + 1) * LANES] = t.astype(BF16)
    sk = ps[:, 512:640]
    sk = sk * css + _swap_halves(sk, SWA_HEAD_DIM // 2) * sns
    sv = ps[:, 640:768]
    lane = lax.broadcasted_iota(jnp.int32, sk.shape, 1)
    lo = lane < SWA_HEAD_DIM
    for src, dst, fill in ((sk, skp_ref, 0.0), (sv, svp_ref, 1.0)):
        sw = pltpu.roll(src, SWA_HEAD_DIM, 1)
        pad_hi = jnp.where(lane == SWA_HEAD_DIM, fill, 0.0)
        pad_lo = jnp.where(lane == 0, fill, 0.0)
        dst[0, :, 0:128] = jnp.where(lo, src, pad_hi).astype(BF16)
        dst[0, :, 128:256] = jnp.where(lo, pad_lo, sw).astype(BF16)
        dst[0, :, 256:384] = jnp.where(lo, sw, pad_hi).astype(BF16)
        dst[0, :, 384:512] = jnp.where(lo, pad_lo, src).astype(BF16)

    c0 += 768
    fu = _dot(hb, w_ref[:, c0:c0 + GROUP_W]).astype(BF16)
    fu_ref[0] = _dot(fu, cdft_ref[...]).astype(BF16)

    c0 += GROUP_W
    pm = _dot(hb, w_ref[:, c0:c0 + 896])
    qn = _rms(pm[:, 0:MLA_Q_LORA], qg_ref[...]).astype(BF16)
    kvn = _rms(pm[:, MLA_Q_LORA:MLA_Q_LORA + MLA_KV_LORA], kvg_ref[...]).astype(BF16)
    csm = csm_ref[...]
    snm = snm_ref[...]

    def rope_m(t):
        lane_m = lax.broadcasted_iota(jnp.int32, t.shape, 1)
        sw = jnp.where(lane_m < MLA_NOPE + MLA_ROPE // 2,
                       pltpu.roll(t, LANES - MLA_ROPE // 2, 1), pltpu.roll(t, MLA_ROPE // 2, 1))
        return t * csm + sw * snm

    q = _dot(qn, wuq_ref[...])
    mscale = (MLA_NOPE + MLA_ROPE) ** -0.5 * LOG2E
    kpe = rope_m(pltpu.roll(pm[:, 768:896], MLA_NOPE, 1))
    kn = _dot(kvn, wuk_ref[...])
    mv = _dot(kvn, wuv_ref[...])
    lane_v = lax.broadcasted_iota(jnp.int32, kpe.shape, 1)
    for hd in range(MLA_HEADS):
        sl = slice(hd * LANES, (hd + 1) * LANES)
        mq_ref[0, :, sl] = (rope_m(q[:, sl]) * mscale).astype(BF16)
        mk_ref[0, :, sl] = (kn[:, sl] + kpe).astype(BF16)
        ones_lane = MLA_V if hd % 2 == 0 else 0
        mvp_ref[0, :, sl] = jnp.where(lane_v == ones_lane, 1.0, mv[:, sl]).astype(BF16)


def _mixer_inputs(x, shift, scale, norm_g, w_in_p, qg, kvg, wuq, wuk, wuv, rope_tabs, cdft):
    b, n, d = x.shape
    tm = min(256, n)
    css, sns, csm, snm = rope_tabs
    row = lambda bi, i: (bi, i, 0)
    const2 = lambda bi, i: (0, 0)
    tab = lambda bi, i: (i, 0)
    outs = [jax.ShapeDtypeStruct((b, n, w), BF16) for w in (512, 512, 512, 512, 512, 1024, 1024, 1024, 1024)]
    return pl.pallas_call(
        _mixin_kernel,
        grid=(b, n // tm),
        in_specs=[
            pl.BlockSpec((1, tm, d), row),
            pl.BlockSpec((1, 1, d), lambda bi, i: (bi, 0, 0)),
            pl.BlockSpec((1, 1, d), lambda bi, i: (bi, 0, 0)),
            pl.BlockSpec((1, d), const2),
            pl.BlockSpec((d, IN_COLS_PAD), const2),
            pl.BlockSpec((1, MLA_Q_LORA), const2),
            pl.BlockSpec((1, MLA_KV_LORA), const2),
            pl.BlockSpec((MLA_Q_LORA, 1024), const2),
            pl.BlockSpec((MLA_KV_LORA, 1024), const2),
            pl.BlockSpec((MLA_KV_LORA, 1024), const2),
            pl.BlockSpec((tm, LANES), tab),
            pl.BlockSpec((tm, LANES), tab),
            pl.BlockSpec((tm, LANES), tab),
            pl.BlockSpec((tm, LANES), tab),
            pl.BlockSpec((GROUP_W, 2 * GROUP_W), const2),
        ],
        out_specs=[pl.BlockSpec((1, tm, o.shape[-1]), row) for o in outs],
        out_shape=outs,
        compiler_params=_cparams(("arbitrary", "arbitrary")),
        name="mixer_inputs",
    )(x, shift, scale, norm_g, w_in_p, qg, kvg, wuq, wuk, wuv, css, sns, csm, snm, cdft)


def _fourier_kernel(ct_ref, st_ref, ab_ref, o_ref):
    acc = _dot(ct_ref[...], ab_ref[0, :, 0:GROUP_W]) + _dot(st_ref[...], ab_ref[0, :, GROUP_W:2 * GROUP_W])
    o_ref[0] = acc.astype(BF16)


def _fourier(ab, ctab, stab):
    b, n, _ = ab.shape
    tk = min(512, n)
    return pl.pallas_call(
        _fourier_kernel,
        grid=(b, n // tk),
        in_specs=[
            pl.BlockSpec((tk, n), lambda bi, i: (i, 0)),
            pl.BlockSpec((tk, n), lambda bi, i: (i, 0)),
            pl.BlockSpec((1, n, 2 * GROUP_W), lambda bi, i: (bi, 0, 0)),
        ],
        out_specs=pl.BlockSpec((1, tk, GROUP_W), lambda bi, i: (bi, i, 0)),
        out_shape=jax.ShapeDtypeStruct((b, n, GROUP_W), BF16),
        compiler_params=_cparams(("arbitrary", "arbitrary")),
        name="fourier",
    )(ctab, stab, ab)


def _swa_kernel(sink_ref, q_ref, kc_ref, vc_ref, *rest, has_band, n_blocks):
    if has_band:
        kp_ref, vp_ref, o_ref = rest
    else:
        (o_ref,) = rest
    i = pl.program_id(1)
    t = SWA_BLOCK
    if has_band:
        start = pl.multiple_of(jnp.clip(i - 1, 0, n_blocks - 3) * t, t)
        qpos = i * t + lax.broadcasted_iota(jnp.int32, (2 * t, 3 * t), 0) % t
        kpos = start + lax.broadcasted_iota(jnp.int32, (2 * t, 3 * t), 1)
        valid = jnp.abs(kpos - qpos) <= SWA_BLOCK
    first = lax.broadcasted_iota(jnp.int32, (2 * t, 1), 0) < t
    lane = lax.broadcasted_iota(jnp.int32, (2 * t, LANES), 1)
    acc = [None] * 4
    for g in range(SWA_KV_HEADS):
        for hh in range(2):
            slot = 2 * g + hh
            sl = slice(slot * LANES, (slot + 1) * LANES)
            cols = (2 * g, 2 * g + 1)
            q2 = jnp.concatenate([q_ref[0, :, c * LANES:(c + 1) * LANES] for c in cols], axis=0)
            sink = jnp.where(first, sink_ref[2 * cols[0] + hh], sink_ref[2 * cols[1] + hh]) * LOG2E
            s_ctx = _dot_t(q2, kc_ref[0, :, sl])
            m = jnp.maximum(jnp.max(s_ctx, axis=-1, keepdims=True), sink)
            if has_band:
                s_b = jnp.where(valid, _dot_t(q2, kp_ref[0, pl.ds(start, 3 * t), sl]), NEG)
                m = jnp.maximum(m, jnp.max(s_b, axis=-1, keepdims=True))
            o = _dot(jnp.exp2(s_ctx - m).astype(BF16), vc_ref[0, :, sl])
            if has_band:
                o = o + _dot(jnp.exp2(s_b - m).astype(BF16), vp_ref[0, pl.ds(start, 3 * t), sl])
            ones_lane = SWA_HEAD_DIM if hh == 0 else 0
            den = o[:, ones_lane:ones_lane + 1] + jnp.exp2(sink - m)
            keep = (lane < SWA_HEAD_DIM) if hh == 0 else (lane >= SWA_HEAD_DIM)
            o = jnp.where(keep, o / den, 0.0)
            for k, c in enumerate(cols):
                part = o[k * t:(k + 1) * t]
                acc[c] = part if acc[c] is None else acc[c] + part
    for c in range(4):
        o_ref[0, :, c * LANES:(c + 1) * LANES] = acc[c].astype(BF16)


def _swa(sink, q, kc, vc, kp=None, vp=None):
    b, n, _ = q.shape
    l = kc.shape[1]
    t = SWA_BLOCK
    has_band = kp is not None
    blk = lambda bi, i: (bi, i, 0)
    whole = lambda bi, i: (bi, 0, 0)
    in_specs = [
        pl.BlockSpec(memory_space=pltpu.SMEM),
        pl.BlockSpec((1, t, 512), blk),
        pl.BlockSpec((1, l, 512), whole),
        pl.BlockSpec((1, l, 512), whole),
    ]
    args = [sink, q, kc, vc]
    if has_band:
        in_specs += [pl.BlockSpec((1, n, 512), whole), pl.BlockSpec((1, n, 512), whole)]
        args += [kp, vp]
    return pl.pallas_call(
        functools.partial(_swa_kernel, has_band=has_band, n_blocks=n // t),
        grid=(b, n // t),
        in_specs=in_specs,
        out_specs=pl.BlockSpec((1, t, 512), blk),
        out_shape=jax.ShapeDtypeStruct((b, n, 512), BF16),
        compiler_params=_cparams(("arbitrary", "arbitrary")),
        name="swa_latent" if has_band else "swa_context",
    )(*args)


def _mla_kernel(q_ref, kc_ref, vc_ref, *rest, has_lat):
    if has_lat:
        k_ref, v_ref, o_ref = rest
    else:
        (o_ref,) = rest
    lane = lax.broadcasted_iota(jnp.int32, (q_ref.shape[1], LANES), 1)
    out = None
    for hh in range(2):
        sl = slice(hh * LANES, (hh + 1) * LANES)
        q = q_ref[0, :, sl]
        s_ctx = _dot_t(q, kc_ref[0, :, sl])
        m = jnp.max(s_ctx, axis=-1, keepdims=True)
        if has_lat:
            s_lat = _dot_t(q, k_ref[0, :, sl])
            m = jnp.maximum(m, jnp.max(s_lat, axis=-1, keepdims=True))
        o = _dot(jnp.exp2(s_ctx - m).astype(BF16), vc_ref[0, :, sl])
        if has_lat:
            o = o + _dot(jnp.exp2(s_lat - m).astype(BF16), v_ref[0, :, sl])
        ones_lane = MLA_V if hh == 0 else 0
        keep = (lane < MLA_V) if hh == 0 else (lane >= MLA_V)
        o = jnp.where(keep, o / o[:, ones_lane:ones_lane + 1], 0.0)
        out = o if out is None else out + o
    o_ref[0] = out.astype(BF16)


def _mla(q, kc, vc, k=None, v=None):
    b, n, _ = q.shape
    l = kc.shape[1]
    tq = min(256, n)
    has_lat = k is not None
    pair = 2 * LANES
    in_specs = [
        pl.BlockSpec((1, tq, pair), lambda bi, h, i: (bi, i, h)),
        pl.BlockSpec((1, l, pair), lambda bi, h, i: (bi, 0, h)),
        pl.BlockSpec((1, l, pair), lambda bi, h, i: (bi, 0, h)),
    ]
    args = [q, kc, vc]
    if has_lat:
        in_specs += [pl.BlockSpec((1, n, pair), lambda bi, h, i: (bi, 0, h)),
                     pl.BlockSpec((1, n, pair), lambda bi, h, i: (bi, 0, h))]
        args += [k, v]
    return pl.pallas_call(
        functools.partial(_mla_kernel, has_lat=has_lat),
        grid=(b, MLA_HEADS // 2, n // tq),
        in_specs=in_specs,
        out_specs=pl.BlockSpec((1, tq, LANES), lambda bi, h, i: (bi, i, h)),
        out_shape=jax.ShapeDtypeStruct((b, n, MLA_HEADS // 2 * LANES), BF16),
        compiler_params=_cparams(("arbitrary", "arbitrary", "arbitrary")),
        name="mla_latent" if has_lat else "mla_context",
    )(*args)


def _mixout_kernel(x_ref, g1_ref, cb_ref, u_ref, up_ref, un_ref, cw_ref, ys_ref, yf_ref, ym_ref, og_ref, wo_ref,
                   n2_ref, sh2_ref, sc2_ref, wr_ref, xo_ref, fx_ref, aff_ref):
    i = pl.program_id(1)
    u = u_ref[0].astype(F32)
    tm = u.shape[0]
    prev = jnp.where(i > 0, up_ref[0].astype(F32)[CONV_HALO - 1:CONV_HALO, :], 0.0)
    nxt = jnp.where(i < pl.num_programs(1) - 1, un_ref[0].astype(F32)[0:1, :], 0.0)
    row = lax.broadcasted_iota(jnp.int32, u.shape, 0)
    u_before = jnp.where(row == 0, prev, pltpu.roll(u, 1, 0))
    u_after = jnp.where(row == tm - 1, nxt, pltpu.roll(u, tm - 1, 0))
    conv = cb_ref[0].astype(F32) * (cw_ref[0:1, :] * u_before + cw_ref[1:2, :] * u + cw_ref[2:3, :] * u_after)
    groups = (conv, ys_ref[0].astype(F32), yf_ref[0].astype(F32), ym_ref[0].astype(F32))
    acc = None
    for gi, y in enumerate(groups):
        sl = slice(gi * GROUP_W, (gi + 1) * GROUP_W)
        yn = _rms(y, og_ref[:, sl]).astype(BF16)
        part = _dot(yn, wo_ref[sl, :])
        acc = part if acc is None else acc + part
    xn = x_ref[0] + g1_ref[0] * acc
    xo_ref[0] = xn
    fx = _rms(xn, n2_ref[...]) * (1.0 + sc2_ref[0]) + sh2_ref[0]
    fx_ref[0] = fx
    fh = fx.astype(BF16)
    fl = (fx - fh.astype(F32)).astype(BF16)
    parts = _dot(fh, wr_ref[...]) + _dot(fl, wr_ref[...])
    logits = parts + pltpu.roll(parts, LANES - N_EXPERTS, 1)
    lane = lax.broadcasted_iota(jnp.int32, logits.shape, 1)
    logits = jnp.where(lane < N_EXPERTS, logits, NEG)
    e = jnp.exp(logits - jnp.max(logits, axis=-1, keepdims=True))
    aff_ref[0] = e / jnp.sum(e, axis=-1, keepdims=True)


def _mixer_out(x, g1, cb, u, conv_w, ys, yf, ym, out_norm_g, w_out_b, norm2_g, sh2, sc2, w_router_p):
    b, n, d = x.shape
    tm = min(256, n)
    halo_per_tile = tm // CONV_HALO
    n_halo = n // CONV_HALO
    row = lambda bi, i: (bi, i, 0)
    per_b = lambda bi, i: (bi, 0, 0)
    const2 = lambda bi, i: (0, 0)
    return pl.pallas_call(
        _mixout_kernel,
        grid=(b, n // tm),
        in_specs=[
            pl.BlockSpec((1, tm, d), row),
            pl.BlockSpec((1, 1, d), per_b),
            pl.BlockSpec((1, tm, GROUP_W), row),
            pl.BlockSpec((1, tm, GROUP_W), row),
            pl.BlockSpec((1, CONV_HALO, GROUP_W), lambda bi, i: (bi, jnp.maximum(i * halo_per_tile - 1, 0), 0)),
            pl.BlockSpec((1, CONV_HALO, GROUP_W),
                         lambda bi, i: (bi, jnp.minimum((i + 1) * halo_per_tile, n_halo - 1), 0)),
            pl.BlockSpec((3, GROUP_W), const2),
            pl.BlockSpec((1, tm, GROUP_W), row),
            pl.BlockSpec((1, tm, GROUP_W), row),
            pl.BlockSpec((1, tm, GROUP_W), row),
            pl.BlockSpec((1, N_GROUPS * GROUP_W), const2),
            pl.BlockSpec((N_GROUPS * GROUP_W, d), const2),
            pl.BlockSpec((1, d), const2),
            pl.BlockSpec((1, 1, d), per_b),
            pl.BlockSpec((1, 1, d), per_b),
            pl.BlockSpec((d, LANES), const2),
        ],
        out_specs=[pl.BlockSpec((1, tm, d), row), pl.BlockSpec((1, tm, d), row),
                   pl.BlockSpec((1, tm, LANES), row)],
        out_shape=[jax.ShapeDtypeStruct((b, n, d), F32), jax.ShapeDtypeStruct((b, n, d), F32),
                   jax.ShapeDtypeStruct((b, n, LANES), F32)],
        compiler_params=_cparams(("arbitrary", "arbitrary")),
        name="mixer_out",
    )(x, g1, cb, u, u, u, conv_w, ys, yf, ym, out_norm_g, w_out_b, norm2_g, sh2, sc2, w_router_p)


def _ffn_kernel(idx_ref, wg_ref, wu_ref, wd_ref, *rest, seg_rows, chunk, n_f, tf):
    n_src = len(seg_rows)
    srcs = rest[:n_src]
    y_ref = rest[n_src]
    stage_ref, xs_ref, act_ref, sem = rest[n_src + 1:]
    s = pl.program_id(1)

    @pl.when(s == 0)
    def _gather():
        base = 0
        for src, rows in zip(srcs, seg_rows):
            for c0 in range(0, rows, chunk):
                cn = min(chunk, rows - c0)
                off = base + c0

                def issue(j, carry, off=off, src=src):
                    r = idx_ref[0, 0, off + j]
                    pltpu.make_async_copy(src.at[pl.ds(r, 1), :], stage_ref.at[pl.ds(j, 1), :], sem).start()
                    return carry

                lax.fori_loop(0, cn, issue, 0, unroll=DMA_ISSUE_UNROLL)
                pltpu.make_async_copy(src.at[pl.ds(0, cn), :], stage_ref.at[pl.ds(0, cn), :], sem).wait()
                xs_ref[off:off + cn, :] = stage_ref[0:cn, :].astype(BF16)
            base += rows

    @pl.when(s < n_f)
    def _gate_up():
        xs = xs_ref[...]
        a = _dot(xs, wg_ref[0, 0].astype(BF16))
        u = _dot(xs, wu_ref[0, 0].astype(BF16))
        act_ref[s] = (a / (1.0 + jnp.exp(-a)) * u).astype(BF16)

    @pl.when(s >= n_f)
    def _down():
        acc = None
        for k in range(n_f):
            part = _dot(act_ref[k], wd_ref[0, 0, k * tf:(k + 1) * tf, :].astype(BF16))
            acc = part if acc is None else acc + part
        y_ref[0] = acc


def _expert_ffn(idx_all, layer, w_gate, w_up, w_down, srcs, seg_rows):
    _, e, d, ff = w_gate.shape
    r = idx_all.shape[-1]
    tf = min(256, ff)
    td = min(256, d)
    n_f = ff // tf
    n_d = d // td
    chunk = min(512, max(seg_rows))
    kern = functools.partial(_ffn_kernel, seg_rows=tuple(seg_rows), chunk=chunk, n_f=n_f, tf=tf)
    return pl.pallas_call(
        kern,
        grid=(e, n_f + n_d),
        in_specs=[
            pl.BlockSpec((1, 1, r), lambda ei, s: (ei, 0, 0), memory_space=pltpu.SMEM),
            pl.BlockSpec((1, 1, d, tf), lambda ei, s: (layer, ei, 0, jnp.minimum(s, n_f - 1))),
            pl.BlockSpec((1, 1, d, tf), lambda ei, s: (layer, ei, 0, jnp.minimum(s, n_f - 1))),
            pl.BlockSpec((1, 1, ff, td), lambda ei, s: (layer, ei, 0, jnp.maximum(s - n_f, 0))),
        ] + [pl.BlockSpec(memory_space=pl.ANY) for _ in srcs],
        out_specs=pl.BlockSpec((1, r, td), lambda ei, s: (ei, 0, jnp.maximum(s - n_f, 0))),
        out_shape=jax.ShapeDtypeStruct((e, r, d), F32),
        scratch_shapes=[
            pltpu.VMEM((chunk, d), F32),
            pltpu.VMEM((r, d), BF16),
            pltpu.VMEM((n_f, r, tf), BF16),
            pltpu.SemaphoreType.DMA(()),
        ],
        compiler_params=_cparams(("arbitrary", "arbitrary")),
        name="expert_ffn",
    )(idx_all, w_gate, w_up, w_down, *srcs)


def _select_kernel(aff_ref, w_ref, idx_ref, bnd_ref, pos_ref, post_ref, *, cap, tile, jchunk):
    n = aff_ref.shape[1]
    bits = pltpu.bitcast(aff_ref[0], jnp.int32)

    def search(it, thr):
        cand = thr | jnp.left_shift(jnp.int32(1), 30 - it)
        cnt = jnp.sum(jnp.where(bits >= cand, 1, 0), axis=0, keepdims=True)
        return jnp.where(cnt >= cap, cand, thr)

    thr = lax.fori_loop(0, 31, search, jnp.zeros((1, LANES), jnp.int32))
    need = (cap - jnp.sum(jnp.where(bits > thr, 1, 0), axis=0, keepdims=True)).astype(F32)

    r = lax.broadcasted_iota(jnp.int32, (tile, tile), 0)
    c = lax.broadcasted_iota(jnp.int32, (tile, tile), 1)
    tri = jnp.where(c <= r, 1.0, 0.0).astype(BF16)

    def chunk(k, carry):
        ceq, cpos = carry
        r0 = pl.multiple_of(k * tile, tile)
        a = aff_ref[0, pl.ds(r0, tile), :]
        bt = pltpu.bitcast(a, jnp.int32)
        eq = bt == thr
        rank = _dot(tri, jnp.where(eq, 1.0, 0.0).astype(BF16)) + ceq
        sel = jnp.where((bt > thr) | (eq & (rank <= need)), 1.0, 0.0)
        pos = _dot(tri, sel.astype(BF16)) + cpos
        pos_ref[pl.ds(r0, tile), :] = pos
        w_ref[0, pl.ds(r0, tile), :] = sel * a
        cpos = pos[tile - 1:tile, :]
        bnd_ref[0, pl.ds(k, 1), :] = cpos.astype(jnp.int32)
        return rank[tile - 1:tile, :], cpos

    zero = jnp.zeros((1, LANES), F32)
    lax.fori_loop(0, n // tile, chunk, (zero, zero))

    for k in range(n // tile):
        post_ref[:, k * tile:(k + 1) * tile] = pos_ref[k * tile:(k + 1) * tile, :].T

    def per_expert(e, carry):
        row = post_ref[pl.ds(e, 1), :]

        def per_j(jc, carry_j):
            j0 = pl.multiple_of(jc * jchunk, jchunk)
            jv = (lax.broadcasted_iota(jnp.int32, (jchunk, n), 0) + j0).astype(F32)
            cnt = jnp.sum(jnp.where(row <= jv, 1.0, 0.0), axis=-1, keepdims=True)
            idx_ref[0, e, pl.ds(j0, jchunk), :] = cnt.astype(jnp.int32)
            return carry_j

        return lax.fori_loop(0, cap // jchunk, per_j, carry)

    lax.fori_loop(0, N_EXPERTS, per_expert, 0)


def _select(aff, cap, tile):
    b, n, _ = aff.shape
    jchunk = min(64, cap)
    kern = functools.partial(_select_kernel, cap=cap, tile=tile, jchunk=jchunk)
    return pl.pallas_call(
        kern,
        grid=(b,),
        in_specs=[pl.BlockSpec((1, n, LANES), lambda bi: (bi, 0, 0))],
        out_specs=[pl.BlockSpec((1, n, LANES), lambda bi: (bi, 0, 0)),
                   pl.BlockSpec((1, N_EXPERTS, cap, 1), lambda bi: (bi, 0, 0, 0)),
                   pl.BlockSpec((1, n // tile, LANES), lambda bi: (bi, 0, 0))],
        out_shape=[jax.ShapeDtypeStruct((b, n, LANES), F32),
                   jax.ShapeDtypeStruct((b, N_EXPERTS, cap, 1), jnp.int32),
                   jax.ShapeDtypeStruct((b, n // tile, LANES), jnp.int32)],
        scratch_shapes=[pltpu.VMEM((n, LANES), F32), pltpu.VMEM((LANES, n), F32)],
        compiler_params=_cparams(("arbitrary",)),
        name="select",
    )(aff)


def _combine_kernel(idx_ref, bnd_ref, x_ref, g2_ref, w_ref, *rest, tile, cap, n_tiles, row0, rows_per_expert,
                    final_norm):
    if final_norm:
        fg_ref, y_hbm, o_ref, buf, cnt_ref, sem = rest
    else:
        y_hbm, o_ref, buf, cnt_ref, sem = rest
    b = pl.program_id(0)
    i = pl.program_id(1)
    plane_rows = N_EXPERTS * tile
    unroll = DMA_ISSUE_UNROLL

    @pl.when((b == 0) & (i == 0))
    def _init():
        buf[...] = jnp.zeros(buf.shape, F32)

    @pl.when(i < n_tiles)
    def _issue():
        slot = i % 2
        dst = buf.at[slot]
        dsem = sem.at[slot]
        t0 = i * tile
        total = jnp.int32(0)
        for e in range(N_EXPERTS):
            lo = bnd_ref[0, i, e]
            hi = bnd_ref[0, i + 1, e]
            src0 = e * rows_per_expert + row0 + b * cap
            groups = lax.shift_right_logical(hi - lo + (unroll - 1), unroll.bit_length() - 1)

            def group(k, carry, e=e, lo=lo, hi=hi, src0=src0):
                for u in range(unroll):
                    j = lo + k * unroll + u
                    ok = j < hi
                    jj = jnp.minimum(j, cap - 1)
                    drow = jnp.where(ok, e * tile + idx_ref[0, e, jj] - t0, plane_rows + unroll * e + u)
                    srow = jnp.where(ok, src0 + jj, 0)
                    pltpu.make_async_copy(y_hbm.at[pl.ds(srow, 1), :], dst.at[pl.ds(drow, 1), :], dsem).start()
                return carry

            lax.fori_loop(0, groups, group, 0)
            total = total + groups * unroll

        pad = total & (SUBLANES - 1)

        def issue_pad(k, carry):
            pltpu.make_async_copy(y_hbm.at[pl.ds(k, 1), :],
                                  dst.at[pl.ds(plane_rows + unroll * N_EXPERTS + k, 1), :], dsem).start()
            return carry

        lax.fori_loop(0, pad, issue_pad, 0)
        cnt_ref[slot] = total + pad

    @pl.when(i > 0)
    def _reduce():
        slot = (i - 1) % 2
        n_rows = pl.multiple_of(cnt_ref[slot], SUBLANES)

        @pl.when(n_rows > 0)
        def _wait():
            pltpu.make_async_copy(y_hbm.at[pl.ds(0, n_rows), :], buf.at[slot, pl.ds(0, n_rows), :],
                                  sem.at[slot]).wait()

        def rows(g, carry):
            r0 = pl.multiple_of(g * COMBINE_ROWS, COMBINE_ROWS)
            wv = w_ref[0, pl.ds(r0, COMBINE_ROWS), :]
            acc = None
            for e in range(N_EXPERTS):
                part = buf[slot, pl.ds(e * tile + r0, COMBINE_ROWS), :] * wv[:, e:e + 1]
                acc = part if acc is None else acc + part
            out = x_ref[0, pl.ds(r0, COMBINE_ROWS), :] + g2_ref[0] * acc
            if final_norm:
                out = _rms(out, fg_ref[...])
            o_ref[0, pl.ds(r0, COMBINE_ROWS), :] = out
            return carry

        lax.fori_loop(0, tile // COMBINE_ROWS, rows, 0)


def _combine(idx, bnd, x, g2, w, y_flat, tile, row0, rows_per_expert, final_g=None):
    b, n, d = x.shape
    cap = idx.shape[-1]
    n_tiles = n // tile
    final_norm = final_g is not None
    kern = functools.partial(_combine_kernel, tile=tile, cap=cap, n_tiles=n_tiles, row0=row0,
                             rows_per_expert=rows_per_expert, final_norm=final_norm)
    row = lambda bi, i: (bi, jnp.maximum(i - 1, 0), 0)
    per_b = lambda bi, i: (bi, 0, 0)
    spare = DMA_ISSUE_UNROLL * N_EXPERTS + SUBLANES
    in_specs = [
        pl.BlockSpec((1, N_EXPERTS, cap), per_b, memory_space=pltpu.SMEM),
        pl.BlockSpec((1, n // tile + 1, N_EXPERTS), per_b, memory_space=pltpu.SMEM),
        pl.BlockSpec((1, tile, d), row),
        pl.BlockSpec((1, 1, d), per_b),
        pl.BlockSpec((1, tile, LANES), row),
    ]
    args = [idx, bnd, x, g2, w]
    if final_norm:
        in_specs.append(pl.BlockSpec((1, d), lambda bi, i: (0, 0)))
        args.append(final_g)
    in_specs.append(pl.BlockSpec(memory_space=pl.ANY))
    args.append(y_flat)
    return pl.pallas_call(
        kern,
        grid=(b, n_tiles + 1),
        in_specs=in_specs,
        out_specs=pl.BlockSpec((1, tile, d), row),
        out_shape=jax.ShapeDtypeStruct((b, n, d), F32),
        scratch_shapes=[pltpu.VMEM((2, N_EXPERTS * tile + spare, d), F32), pltpu.SMEM((2,), jnp.int32),
                        pltpu.SemaphoreType.DMA((2,))],
        compiler_params=_cparams(("arbitrary", "arbitrary")),
        name="combine",
    )(*args)


def _route(aff, cap, tile, row_stride):
    b = aff.shape[0]
    w, idx, bnd = _select(aff, cap, tile)
    idx = idx.reshape(b, N_EXPERTS, cap)
    bnd = jnp.pad(bnd[:, :, :N_EXPERTS], ((0, 0), (1, 0), (0, 0)))
    rows = idx + (jnp.arange(b, dtype=jnp.int32) * row_stride)[:, None, None]
    rows = jnp.swapaxes(rows, 0, 1).reshape(N_EXPERTS, b * cap)
    return w, idx, bnd, rows


def _axial_tables(n_tok, rot_dim):
    rows = (jnp.arange(n_tok, dtype=jnp.int32) // GRID_W).astype(F32)
    cols = (jnp.arange(n_tok, dtype=jnp.int32) % GRID_W).astype(F32)
    n_freq = rot_dim // 4
    inv_freq = ROPE_THETA ** (-jnp.arange(n_freq, dtype=F32) / n_freq)
    ang = jnp.concatenate([rows[:, None] * inv_freq, cols[:, None] * inv_freq], axis=-1)
    return jnp.cos(ang), jnp.sin(ang)


def _rope_tables(n_tok, identity):
    if identity:
        one = jnp.ones((n_tok, LANES), F32)
        zero = jnp.zeros((n_tok, LANES), F32)
        return one, zero, one, zero
    cos_s, sin_s = _axial_tables(n_tok, SWA_HEAD_DIM)
    css = jnp.tile(jnp.concatenate([cos_s, cos_s], axis=-1), (1, 2))
    sns = jnp.tile(jnp.concatenate([-sin_s, sin_s], axis=-1), (1, 2))
    cos_m, sin_m = _axial_tables(n_tok, MLA_ROPE)
    ones = jnp.ones((n_tok, MLA_NOPE), F32)
    tail = LANES - MLA_NOPE - MLA_ROPE
    csm = jnp.concatenate([ones, cos_m, cos_m, jnp.ones((n_tok, tail), F32)], axis=-1)
    snm = jnp.concatenate([0.0 * ones, -sin_m, sin_m, jnp.zeros((n_tok, tail), F32)], axis=-1)
    return css, sns, csm, snm


def _mla_weight_slots(w_uq, w_ukv):
    dq = MLA_NOPE + MLA_ROPE
    wuq = jnp.pad(w_uq.reshape(MLA_Q_LORA, MLA_HEADS, dq), ((0, 0), (0, 0), (0, LANES - dq)))
    wkv = w_ukv.reshape(MLA_KV_LORA, MLA_HEADS, MLA_NOPE + MLA_V)
    wuk = jnp.pad(wkv[..., :MLA_NOPE], ((0, 0), (0, 0), (0, LANES - MLA_NOPE)))
    wv = wkv[..., MLA_NOPE:]
    even = jnp.pad(wv, ((0, 0), (0, 0), (0, LANES - MLA_V)))
    odd = jnp.pad(wv, ((0, 0), (0, 0), (LANES - MLA_V, 0)))
    is_even = (jnp.arange(MLA_HEADS) % 2 == 0)[None, :, None]
    wuv = jnp.where(is_even, even, odd)
    flat = lambda w: w.reshape(w.shape[0], MLA_HEADS * LANES).astype(BF16)
    return flat(wuq), flat(wuk), flat(wuv)


def _phase_tables(k, m, period):
    ang = ((k[:, None] * m[None, :]) % period).astype(F32) * (2.0 * math.pi / period)
    return jnp.cos(ang), jnp.sin(ang)


def _dft_tables(n_tok):
    k = jnp.arange(n_tok, dtype=jnp.int32)
    side = math.isqrt(n_tok)
    if side * side != n_tok:
        c, s = _phase_tables(k, k, n_tok)
        return c.astype(BF16), s.astype(BF16)
    m = jnp.arange(side, dtype=jnp.int32)
    ca, sa = _phase_tables(k, m * side, n_tok)
    cb, sb = _phase_tables(k, m, n_tok)
    c = ca[:, :, None] * cb[:, None, :] - sa[:, :, None] * sb[:, None, :]
    s = sa[:, :, None] * cb[:, None, :] + ca[:, :, None] * sb[:, None, :]
    return c.reshape(n_tok, n_tok).astype(BF16), s.reshape(n_tok, n_tok).astype(BF16)


def _channel_dft(n_tok):
    ch = jnp.arange(GROUP_W, dtype=jnp.int32)
    c, s = _phase_tables(ch % FNET_HEAD_DIM, ch % FNET_HEAD_DIM, FNET_HEAD_DIM)
    same = (ch[:, None] // FNET_HEAD_DIM) == (ch[None, :] // FNET_HEAD_DIM)
    scale = (n_tok * FNET_HEAD_DIM) ** -0.5
    return (jnp.concatenate([jnp.where(same, c, 0.0), jnp.where(same, -s, 0.0)], axis=1) * scale).astype(BF16)


def kernel(x, c, ctx, c_ctx, ada_w, ada_b, norm1_g, norm2_g, w_in, conv_w, swa_sink, mla_q_norm_g, mla_w_uq,
           mla_kv_norm_g, mla_w_ukv, out_norm_g, w_out, w_router, w_gate, w_up, w_down, final_norm_g):
    b, n, d = x.shape
    l = ctx.shape[1]
    depth = ada_w.shape[0]
    cap_x = EC_CAPACITY * n // N_EXPERTS
    cap_c = EC_CAPACITY * l // N_EXPERTS
    tile_x = min(128, n)
    tile_c = min(128, l)

    c_rows = jnp.concatenate([c, c_ctx[None, :], jnp.zeros((-(b + 1) % 8, d), F32)], axis=0)
    mods = _adaln(c_rows, ada_w, ada_b)
    rope_x = _rope_tables(n, identity=False)
    rope_c = _rope_tables(l, identity=True)
    dft_x, dft_c = _dft_tables(n), _dft_tables(l)
    cdft_x, cdft_c = _channel_dft(n), _channel_dft(l)

    for layer in range(depth):
        last = layer == depth - 1
        mx = mods[layer, :b].reshape(b, 1, 6, d)
        mc = jnp.broadcast_to(mods[layer, b].reshape(1, 1, 6, d), (b, 1, 6, d))
        sh1, sc1, g1, sh2, sc2, g2 = [mx[:, :, k] for k in range(6)]
        csh1, csc1, cg1, csh2, csc2, cg2 = [mc[:, :, k] for k in range(6)]

        w_in_p = jnp.pad(w_in[layer].astype(BF16), ((0, 0), (0, IN_COLS_PAD - IN_COLS)))
        wuq, wuk, wuv = _mla_weight_slots(mla_w_uq[layer], mla_w_ukv[layer])
        n1 = norm1_g[layer][None, :]
        qg = mla_q_norm_g[layer][None, :]
        kvg = mla_kv_norm_g[layer][None, :]
        px = _mixer_inputs(x, sh1, sc1, n1, w_in_p, qg, kvg, wuq, wuk, wuv, rope_x, cdft_x)
        pc = _mixer_inputs(ctx, csh1, csc1, n1, w_in_p, qg, kvg, wuq, wuk, wuv, rope_c, cdft_c)
        cb, u, sq, skp, svp, fu, mq, mk, mvp = px
        ccb, cu, csq, cskp, csvp, cfu, cmq, cmk, cmvp = pc

        sink = swa_sink[layer]
        og = out_norm_g[layer][None, :]
        wo = w_out[layer].astype(BF16)
        n2 = norm2_g[layer][None, :]
        wr_hi = w_router[layer].astype(BF16)
        wr_lo = (w_router[layer] - wr_hi.astype(F32)).astype(BF16)
        wr = jnp.pad(jnp.concatenate([wr_hi, wr_lo], axis=1), ((0, 0), (0, LANES - 2 * N_EXPERTS)))

        yx = (_swa(sink, sq, cskp, csvp, skp, svp), _fourier(fu, *dft_x), _mla(mq, cmk, cmvp, mk, mvp))
        x, fx, aff = _mixer_out(x, g1, cb, u, conv_w[layer], *yx, og, wo, n2, sh2, sc2, wr)
        srcs = [fx.reshape(b * n, d)]
        seg_rows = [b * cap_x]
        wx, idx_x, bnd_x, rows_x = _route(aff, cap_x, tile_x, n)
        row_parts = [rows_x]
        if not last:
            yc = (_swa(sink, csq, cskp, csvp), _fourier(cfu, *dft_c), _mla(cmq, cmk, cmvp))
            ctx, fc, affc = _mixer_out(ctx, cg1, ccb, cu, conv_w[layer], *yc, og, wo, n2, csh2, csc2, wr)
            srcs.append(fc.reshape(b * l, d))
            seg_rows.append(b * cap_c)
            wc, idx_c, bnd_c, rows_c = _route(affc, cap_c, tile_c, l)
            row_parts.append(rows_c)
        idx_all = jnp.concatenate(row_parts, axis=1)[:, None, :]
        y = _expert_ffn(idx_all, layer, w_gate, w_up, w_down, srcs, seg_rows)
        rows_per_expert = y.shape[1]
        y_flat = y.reshape(N_EXPERTS * rows_per_expert, d)
        x = _combine(idx_x, bnd_x, x, g2, wx, y_flat, tile_x, 0, rows_per_expert,
                     final_g=final_norm_g[None, :] if last else None)
        if not last:
            ctx = _combine(idx_c, bnd_c, ctx, cg2, wc, y_flat, tile_c, b * cap_x, rows_per_expert)
    return x
```

```python
import functools
import math

import jax
import jax.numpy as jnp
from jax import lax
from jax.experimental import pallas as pl
from jax.experimental.pallas import tpu as pltpu

F32 = jnp.float32
BF16 = jnp.bfloat16
HIGHEST = lax.Precision.HIGHEST

GRID_W = 64
GROUP_W = 512
N_GROUPS = 4
SWA_HEAD_DIM = 64
SWA_HEADS = 8
SWA_KV_HEADS = 2
SWA_BLOCK = 128
FNET_HEAD_DIM = 64
MLA_HEADS = 8
MLA_NOPE = 64
MLA_ROPE = 32
MLA_V = 64
MLA_Q_LORA = 512
MLA_KV_LORA = 256
N_EXPERTS = 16
EC_CAPACITY = 2
ROPE_THETA = 10000.0
EPS = 1e-6
IN_COLS = 3616
IN_COLS_PAD = 3712
LANES = 128
SUBLANES = 8
DMA_ISSUE_UNROLL = 4
COMBINE_ROWS = 16
CONV_HALO = 16
NEG = -1e30
LOG2E = 1.4426950408889634

VMEM_LIMIT = 60 * 1024 * 1024


def _cparams(sem):
    return pltpu.CompilerParams(dimension_semantics=sem, vmem_limit_bytes=VMEM_LIMIT)


def _dot(a, b):
    return jnp.dot(a, b, preferred_element_type=F32)


def _dot_t(a, b):
    return lax.dot_general(a, b, (((1,), (1,)), ((), ())), preferred_element_type=F32)


def _rms(x, g):
    return x * lax.rsqrt(jnp.mean(x * x, axis=-1, keepdims=True) + EPS) * g


def _slab_rows(d):
    return d // (2 * LANES)


def _pack_pair(hi, lo):
    hb = lax.bitcast_convert_type(hi.astype(BF16).astype(F32), jnp.uint32)
    lb = lax.bitcast_convert_type(lo.astype(BF16).astype(F32), jnp.uint32)
    return hb | (lb >> 16)


def _unpack_pair(v):
    hi = lax.bitcast_convert_type(v & jnp.uint32(0xFFFF0000), F32)
    lo = lax.bitcast_convert_type(v << 16, F32)
    return hi, lo


def _store_slabs(ref, lead, x, first_chunk=0):
    rows, width = x.shape
    half = width // 2
    s_rows = ref.shape[-2] // rows
    for c in range(half // LANES):
        packed = _pack_pair(x[:, c * LANES:(c + 1) * LANES], x[:, half + c * LANES:half + (c + 1) * LANES])
        ref[(*lead, pl.ds(first_chunk + c, rows, stride=s_rows), slice(None))] = packed


def _adaln_kernel(c_ref, w_ref, b_ref, o_ref):
    c = c_ref[...]
    s = c / (1.0 + jnp.exp(-c))
    o_ref[0] = lax.dot_general(s, w_ref[0], (((1,), (0,)), ((), ())),
                               precision=HIGHEST, preferred_element_type=F32) + b_ref[0]


def _adaln(c_rows, ada_w, ada_b):
    depth, d, d6 = ada_w.shape
    tn = 1024 if d6 % 1024 == 0 else d6
    rows = c_rows.shape[0]
    return pl.pallas_call(
        _adaln_kernel,
        grid=(depth, d6 // tn),
        in_specs=[
            pl.BlockSpec((rows, d), lambda l, j: (0, 0)),
            pl.BlockSpec((1, d, tn), lambda l, j: (l, 0, j)),
            pl.BlockSpec((1, 1, tn), lambda l, j: (l, 0, j)),
        ],
        out_specs=pl.BlockSpec((1, rows, tn), lambda l, j: (l, 0, j)),
        out_shape=jax.ShapeDtypeStruct((depth, rows, d6), F32),
        compiler_params=_cparams(("arbitrary", "arbitrary")),
        name="adaln",
    )(c_rows, ada_w, ada_b.reshape(depth, 1, d6))


def _swap_halves(x, width):
    lane = lax.broadcasted_iota(jnp.int32, x.shape, 1)
    first = (lane % (2 * width)) < width
    return jnp.where(first, pltpu.roll(x, LANES - width, 1), pltpu.roll(x, width, 1))


def _mixin_kernel(x_ref, sh_ref, sc_ref, g_ref, w_ref, qg_ref, kvg_ref, wuq_ref, wuk_ref, wuv_ref,
                  css_ref, sns_ref, csm_ref, snm_ref, cdft_ref,
                  cb_ref, u_ref, sq_ref, skp_ref, svp_ref, fu_ref, mq_ref, mk_ref, mvp_ref):
    x = x_ref[0]
    h = _rms(x, g_ref[...]) * (1.0 + sc_ref[0]) + sh_ref[0]
    hb = h.astype(BF16)

    pc = _dot(hb, w_ref[:, 0:3 * GROUP_W])
    cb_ref[0] = pc[:, 0:GROUP_W].astype(BF16)
    u_ref[0] = (pc[:, GROUP_W:2 * GROUP_W] * pc[:, 2 * GROUP_W:3 * GROUP_W]).astype(BF16)

    c0 = 3 * GROUP_W
    ps = _dot(hb, w_ref[:, c0:c0 + 768])
    css = css_ref[...]
    sns = sns_ref[...]
    qscale = SWA_HEAD_DIM ** -0.5 * LOG2E
    for c in range(4):
        t = ps[:, c * LANES:(c + 1) * LANES]
        t = (t * css + _swap_halves(t, SWA_HEAD_DIM // 2) * sns) * qscale
        sq_ref[0, :, c * LANES:(c + 1) * LANES] = t.astype(BF16)
    sk = ps[:, 512:640]
    sk = sk * css + _swap_halves(sk, SWA_HEAD_DIM // 2) * sns
    sv = ps[:, 640:768]
    lane = lax.broadcasted_iota(jnp.int32, sk.shape, 1)
    lo = lane < SWA_HEAD_DIM
    for src, dst, fill in ((sk, skp_ref, 0.0), (sv, svp_ref, 1.0)):
        sw = pltpu.roll(src, SWA_HEAD_DIM, 1)
        pad_hi = jnp.where(lane == SWA_HEAD_DIM, fill, 0.0)
        pad_lo = jnp.where(lane == 0, fill, 0.0)
        dst[0, :, 0:128] = jnp.where(lo, src, pad_hi).astype(BF16)
        dst[0, :, 128:256] = jnp.where(lo, pad_lo, sw).astype(BF16)
        dst[0, :, 256:384] = jnp.where(lo, sw, pad_hi).astype(BF16)
        dst[0, :, 384:512] = jnp.where(lo, pad_lo, src).astype(BF16)

    c0 += 768
    fu = _dot(hb, w_ref[:, c0:c0 + GROUP_W]).astype(BF16)
    fu_ref[0] = _dot(fu, cdft_ref[...]).astype(BF16)

    c0 += GROUP_W
    pm = _dot(hb, w_ref[:, c0:c0 + 896])
    qn = _rms(pm[:, 0:MLA_Q_LORA], qg_ref[...]).astype(BF16)
    kvn = _rms(pm[:, MLA_Q_LORA:MLA_Q_LORA + MLA_KV_LORA], kvg_ref[...]).astype(BF16)
    csm = csm_ref[...]
    snm = snm_ref[...]

    def rope_m(t):
        lane_m = lax.broadcasted_iota(jnp.int32, t.shape, 1)
        sw = jnp.where(lane_m < MLA_NOPE + MLA_ROPE // 2,
                       pltpu.roll(t, LANES - MLA_ROPE // 2, 1), pltpu.roll(t, MLA_ROPE // 2, 1))
        return t * csm + sw * snm

    q = _dot(qn, wuq_ref[...])
    mscale = (MLA_NOPE + MLA_ROPE) ** -0.5 * LOG2E
    kpe = rope_m(pltpu.roll(pm[:, 768:896], MLA_NOPE, 1))
    kn = _dot(kvn, wuk_ref[...])
    mv = _dot(kvn, wuv_ref[...])
    lane_v = lax.broadcasted_iota(jnp.int32, kpe.shape, 1)
    for hd in range(MLA_HEADS):
        sl = slice(hd * LANES, (hd + 1) * LANES)
        mq_ref[0, :, sl] = (rope_m(q[:, sl]) * mscale).astype(BF16)
        mk_ref[0, :, sl] = (kn[:, sl] + kpe).astype(BF16)
        ones_lane = MLA_V if hd % 2 == 0 else 0
        mvp_ref[0, :, sl] = jnp.where(lane_v == ones_lane, 1.0, mv[:, sl]).astype(BF16)


def _mixer_inputs(x, shift, scale, norm_g, w_in_p, qg, kvg, wuq, wuk, wuv, rope_tabs, cdft):
    b, n, d = x.shape
    tm = min(256, n)
    css, sns, csm, snm = rope_tabs
    row = lambda bi, i: (bi, i, 0)
    const2 = lambda bi, i: (0, 0)
    tab = lambda bi, i: (i, 0)
    outs = [jax.ShapeDtypeStruct((b, n, w), BF16) for w in (512, 512, 512, 512, 512, 1024, 1024, 1024, 1024)]
    return pl.pallas_call(
        _mixin_kernel,
        grid=(b, n // tm),
        in_specs=[
            pl.BlockSpec((1, tm, d), row),
            pl.BlockSpec((1, 1, d), lambda bi, i: (bi, 0, 0)),
            pl.BlockSpec((1, 1, d), lambda bi, i: (bi, 0, 0)),
            pl.BlockSpec((1, d), const2),
            pl.BlockSpec((d, IN_COLS_PAD), const2),
            pl.BlockSpec((1, MLA_Q_LORA), const2),
            pl.BlockSpec((1, MLA_KV_LORA), const2),
            pl.BlockSpec((MLA_Q_LORA, 1024), const2),
            pl.BlockSpec((MLA_KV_LORA, 1024), const2),
            pl.BlockSpec((MLA_KV_LORA, 1024), const2),
            pl.BlockSpec((tm, LANES), tab),
            pl.BlockSpec((tm, LANES), tab),
            pl.BlockSpec((tm, LANES), tab),
            pl.BlockSpec((tm, LANES), tab),
            pl.BlockSpec((GROUP_W, 2 * GROUP_W), const2),
        ],
        out_specs=[pl.BlockSpec((1, tm, o.shape[-1]), row) for o in outs],
        out_shape=outs,
        compiler_params=_cparams(("arbitrary", "arbitrary")),
        name="mixer_inputs",
    )(x, shift, scale, norm_g, w_in_p, qg, kvg, wuq, wuk, wuv, css, sns, csm, snm, cdft)


def _fourier_kernel(ct_ref, st_ref, ab_ref, o_ref):
    acc = _dot(ct_ref[...], ab_ref[0, :, 0:GROUP_W]) + _dot(st_ref[...], ab_ref[0, :, GROUP_W:2 * GROUP_W])
    o_ref[0] = acc.astype(BF16)


def _fourier(ab, ctab, stab):
    b, n, _ = ab.shape
    tk = min(512, n)
    return pl.pallas_call(
        _fourier_kernel,
        grid=(b, n // tk),
        in_specs=[
            pl.BlockSpec((tk, n), lambda bi, i: (i, 0)),
            pl.BlockSpec((tk, n), lambda bi, i: (i, 0)),
            pl.BlockSpec((1, n, 2 * GROUP_W), lambda bi, i: (bi, 0, 0)),
        ],
        out_specs=pl.BlockSpec((1, tk, GROUP_W), lambda bi, i: (bi, i, 0)),
        out_shape=jax.ShapeDtypeStruct((b, n, GROUP_W), BF16),
        compiler_params=_cparams(("arbitrary", "arbitrary")),
        name="fourier",
    )(ctab, stab, ab)


def _swa_kernel(sink_ref, q_ref, kc_ref, vc_ref, *rest, has_band, n_blocks):
    if has_band:
        kp_ref, vp_ref, o_ref = rest
    else:
        (o_ref,) = rest
    i = pl.program_id(1)
    t = SWA_BLOCK
    if has_band:
        start = pl.multiple_of(jnp.clip(i - 1, 0, n_blocks - 3) * t, t)
        qpos = i * t + lax.broadcasted_iota(jnp.int32, (2 * t, 3 * t), 0) % t
        kpos = start + lax.broadcasted_iota(jnp.int32, (2 * t, 3 * t), 1)
        valid = jnp.abs(kpos - qpos) <= SWA_BLOCK
    first = lax.broadcasted_iota(jnp.int32, (2 * t, 1), 0) < t
    lane = lax.broadcasted_iota(jnp.int32, (2 * t, LANES), 1)
    acc = [None] * 4
    for g in range(SWA_KV_HEADS):
        for hh in range(2):
            slot = 2 * g + hh
            sl = slice(slot * LANES, (slot + 1) * LANES)
            cols = (2 * g, 2 * g + 1)
            q2 = jnp.concatenate([q_ref[0, :, c * LANES:(c + 1) * LANES] for c in cols], axis=0)
            sink = jnp.where(first, sink_ref[2 * cols[0] + hh], sink_ref[2 * cols[1] + hh]) * LOG2E
            s_ctx = _dot_t(q2, kc_ref[0, :, sl])
            m = jnp.maximum(jnp.max(s_ctx, axis=-1, keepdims=True), sink)
            if has_band:
                s_b = jnp.where(valid, _dot_t(q2, kp_ref[0, pl.ds(start, 3 * t), sl]), NEG)
                m = jnp.maximum(m, jnp.max(s_b, axis=-1, keepdims=True))
            o = _dot(jnp.exp2(s_ctx - m).astype(BF16), vc_ref[0, :, sl])
            if has_band:
                o = o + _dot(jnp.exp2(s_b - m).astype(BF16), vp_ref[0, pl.ds(start, 3 * t), sl])
            ones_lane = SWA_HEAD_DIM if hh == 0 else 0
            den = o[:, ones_lane:ones_lane + 1] + jnp.exp2(sink - m)
            keep = (lane < SWA_HEAD_DIM) if hh == 0 else (lane >= SWA_HEAD_DIM)
            o = jnp.where(keep, o / den, 0.0)
            for k, c in enumerate(cols):
                part = o[k * t:(k + 1) * t]
                acc[c] = part if acc[c] is None else acc[c] + part
    for c in range(4):
        o_ref[0, :, c * LANES:(c + 1) * LANES] = acc[c].astype(BF16)


def _swa(sink, q, kc, vc, kp=None, vp=None):
    b, n, _ = q.shape
    l = kc.shape[1]
    t = SWA_BLOCK
    has_band = kp is not None
    blk = lambda bi, i: (bi, i, 0)
    whole = lambda bi, i: (bi, 0, 0)
    in_specs = [
        pl.BlockSpec(memory_space=pltpu.SMEM),
        pl.BlockSpec((1, t, 512), blk),
        pl.BlockSpec((1, l, 512), whole),
        pl.BlockSpec((1, l, 512), whole),
    ]
    args = [sink, q, kc, vc]
    if has_band:
        in_specs += [pl.BlockSpec((1, n, 512), whole), pl.BlockSpec((1, n, 512), whole)]
        args += [kp, vp]
    return pl.pallas_call(
        functools.partial(_swa_kernel, has_band=has_band, n_blocks=n // t),
        grid=(b, n // t),
        in_specs=in_specs,
        out_specs=pl.BlockSpec((1, t, 512), blk),
        out_shape=jax.ShapeDtypeStruct((b, n, 512), BF16),
        compiler_params=_cparams(("arbitrary", "arbitrary")),
        name="swa_latent" if has_band else "swa_context",
    )(*args)


def _mla_kernel(q_ref, kc_ref, vc_ref, *rest, has_lat):
    if has_lat:
        k_ref, v_ref, o_ref = rest
    else:
        (o_ref,) = rest
    lane = lax.broadcasted_iota(jnp.int32, (q_ref.shape[1], LANES), 1)
    out = None
    for hh in range(2):
        sl = slice(hh * LANES, (hh + 1) * LANES)
        q = q_ref[0, :, sl]
        s_ctx = _dot_t(q, kc_ref[0, :, sl])
        m = jnp.max(s_ctx, axis=-1, keepdims=True)
        if has_lat:
            s_lat = _dot_t(q, k_ref[0, :, sl])
            m = jnp.maximum(m, jnp.max(s_lat, axis=-1, keepdims=True))
        o = _dot(jnp.exp2(s_ctx - m).astype(BF16), vc_ref[0, :, sl])
        if has_lat:
            o = o + _dot(jnp.exp2(s_lat - m).astype(BF16), v_ref[0, :, sl])
        ones_lane = MLA_V if hh == 0 else 0
        keep = (lane < MLA_V) if hh == 0 else (lane >= MLA_V)
        o = jnp.where(keep, o / o[:, ones_lane:ones_lane + 1], 0.0)
        out = o if out is None else out + o
    o_ref[0] = out.astype(BF16)


def _mla(q, kc, vc, k=None, v=None):
    b, n, _ = q.shape
    l = kc.shape[1]
    tq = min(256, n)
    has_lat = k is not None
    pair = 2 * LANES
    in_specs = [
        pl.BlockSpec((1, tq, pair), lambda bi, h, i: (bi, i, h)),
        pl.BlockSpec((1, l, pair), lambda bi, h, i: (bi, 0, h)),
        pl.BlockSpec((1, l, pair), lambda bi, h, i: (bi, 0, h)),
    ]
    args = [q, kc, vc]
    if has_lat:
        in_specs += [pl.BlockSpec((1, n, pair), lambda bi, h, i: (bi, 0, h)),
                     pl.BlockSpec((1, n, pair), lambda bi, h, i: (bi, 0, h))]
        args += [k, v]
    return pl.pallas_call(
        functools.partial(_mla_kernel, has_lat=has_lat),
        grid=(b, MLA_HEADS // 2, n // tq),
        in_specs=in_specs,
        out_specs=pl.BlockSpec((1, tq, LANES), lambda bi, h, i: (bi, i, h)),
        out_shape=jax.ShapeDtypeStruct((b, n, MLA_HEADS // 2 * LANES), BF16),
        compiler_params=_cparams(("arbitrary", "arbitrary", "arbitrary")),
        name="mla_latent" if has_lat else "mla_context",
    )(*args)


def _mixout_kernel(x_ref, g1_ref, cb_ref, u_ref, up_ref, un_ref, cw_ref, ys_ref, yf_ref, ym_ref, og_ref, wo_ref,
                   n2_ref, sh2_ref, sc2_ref, wr_ref, xo_ref, fx_ref, aff_ref):
    i = pl.program_id(1)
    u = u_ref[0].astype(F32)
    tm = u.shape[0]
    prev = jnp.where(i > 0, up_ref[0].astype(F32)[CONV_HALO - 1:CONV_HALO, :], 0.0)
    nxt = jnp.where(i < pl.num_programs(1) - 1, un_ref[0].astype(F32)[0:1, :], 0.0)
    row = lax.broadcasted_iota(jnp.int32, u.shape, 0)
    u_before = jnp.where(row == 0, prev, pltpu.roll(u, 1, 0))
    u_after = jnp.where(row == tm - 1, nxt, pltpu.roll(u, tm - 1, 0))
    conv = cb_ref[0].astype(F32) * (cw_ref[0:1, :] * u_before + cw_ref[1:2, :] * u + cw_ref[2:3, :] * u_after)
    groups = (conv, ys_ref[0].astype(F32), yf_ref[0].astype(F32), ym_ref[0].astype(F32))
    acc = None
    for gi, y in enumerate(groups):
        sl = slice(gi * GROUP_W, (gi + 1) * GROUP_W)
        yn = _rms(y, og_ref[:, sl]).astype(BF16)
        part = _dot(yn, wo_ref[sl, :])
        acc = part if acc is None else acc + part
    xn = x_ref[0] + g1_ref[0] * acc
    xo_ref[0] = xn
    fx = _rms(xn, n2_ref[...]) * (1.0 + sc2_ref[0]) + sh2_ref[0]
    _store_slabs(fx_ref, (0,), fx)
    fh = fx.astype(BF16)
    fl = (fx - fh.astype(F32)).astype(BF16)
    parts = _dot(fh, wr_ref[...]) + _dot(fl, wr_ref[...])
    logits = parts + pltpu.roll(parts, LANES - N_EXPERTS, 1)
    lane = lax.broadcasted_iota(jnp.int32, logits.shape, 1)
    logits = jnp.where(lane < N_EXPERTS, logits, NEG)
    e = jnp.exp(logits - jnp.max(logits, axis=-1, keepdims=True))
    aff_ref[0] = e / jnp.sum(e, axis=-1, keepdims=True)


def _mixer_out(x, g1, cb, u, conv_w, ys, yf, ym, out_norm_g, w_out_b, norm2_g, sh2, sc2, w_router_p):
    b, n, d = x.shape
    tm = min(256, n)
    halo_per_tile = tm // CONV_HALO
    n_halo = n // CONV_HALO
    row = lambda bi, i: (bi, i, 0)
    per_b = lambda bi, i: (bi, 0, 0)
    const2 = lambda bi, i: (0, 0)
    return pl.pallas_call(
        _mixout_kernel,
        grid=(b, n // tm),
        in_specs=[
            pl.BlockSpec((1, tm, d), row),
            pl.BlockSpec((1, 1, d), per_b),
            pl.BlockSpec((1, tm, GROUP_W), row),
            pl.BlockSpec((1, tm, GROUP_W), row),
            pl.BlockSpec((1, CONV_HALO, GROUP_W), lambda bi, i: (bi, jnp.maximum(i * halo_per_tile - 1, 0), 0)),
            pl.BlockSpec((1, CONV_HALO, GROUP_W),
                         lambda bi, i: (bi, jnp.minimum((i + 1) * halo_per_tile, n_halo - 1), 0)),
            pl.BlockSpec((3, GROUP_W), const2),
            pl.BlockSpec((1, tm, GROUP_W), row),
            pl.BlockSpec((1, tm, GROUP_W), row),
            pl.BlockSpec((1, tm, GROUP_W), row),
            pl.BlockSpec((1, N_GROUPS * GROUP_W), const2),
            pl.BlockSpec((N_GROUPS * GROUP_W, d), const2),
            pl.BlockSpec((1, d), const2),
            pl.BlockSpec((1, 1, d), per_b),
            pl.BlockSpec((1, 1, d), per_b),
            pl.BlockSpec((d, LANES), const2),
        ],
        out_specs=[pl.BlockSpec((1, tm, d), row), pl.BlockSpec((1, tm * _slab_rows(d), LANES), row),
                   pl.BlockSpec((1, tm, LANES), row)],
        out_shape=[jax.ShapeDtypeStruct((b, n, d), F32),
                   jax.ShapeDtypeStruct((b, n * _slab_rows(d), LANES), jnp.uint32),
                   jax.ShapeDtypeStruct((b, n, LANES), F32)],
        compiler_params=_cparams(("arbitrary", "arbitrary")),
        name="mixer_out",
    )(x, g1, cb, u, u, u, conv_w, ys, yf, ym, out_norm_g, w_out_b, norm2_g, sh2, sc2, w_router_p)


def _ffn_kernel(idx_ref, wg_ref, wu_ref, wda_ref, wdb_ref, *rest, seg_rows, chunk, n_f, tf, td, n_d2):
    n_src = len(seg_rows)
    srcs = rest[:n_src]
    y_hbm = rest[n_src]
    stage_ref, xs_ref, act_ref, ybuf, sem, ysem = rest[n_src + 1:]
    e = pl.program_id(0)
    s = pl.program_id(1)
    r, d = xs_ref.shape
    sr = _slab_rows(d)
    half = d // 2
    y_dst = y_hbm.at[pl.ds(pl.multiple_of(e * (r * sr), SUBLANES), r * sr), :]

    @pl.when(s == 0)
    def _gather():
        base = 0
        for src, rows in zip(srcs, seg_rows):
            for c0 in range(0, rows, chunk):
                cn = min(chunk, rows - c0)
                off = base + c0

                def issue(j, carry, off=off, src=src):
                    tok = pl.multiple_of(idx_ref[0, 0, off + j] * sr, sr)
                    pltpu.make_async_copy(src.at[pl.ds(tok, sr), :],
                                          stage_ref.at[pl.ds(pl.multiple_of(j * sr, sr), sr), :], sem).start()
                    return carry

                lax.fori_loop(0, cn, issue, 0, unroll=DMA_ISSUE_UNROLL)
                pltpu.make_async_copy(src.at[pl.ds(0, cn * sr), :], stage_ref.at[pl.ds(0, cn * sr), :], sem).wait()
                for c in range(sr):
                    hi, lo = _unpack_pair(stage_ref[pl.ds(c, cn, stride=sr), :])
                    xs_ref[off:off + cn, c * LANES:(c + 1) * LANES] = hi.astype(BF16)
                    xs_ref[off:off + cn, half + c * LANES:half + (c + 1) * LANES] = lo.astype(BF16)
            base += rows

    @pl.when(s < n_f)
    def _gate_up():
        xs = xs_ref[...]
        a = _dot(xs, wg_ref[0, 0].astype(BF16))
        u = _dot(xs, wu_ref[0, 0].astype(BF16))
        act_ref[s] = (a / (1.0 + jnp.exp(-a)) * u).astype(BF16)

    @pl.when(s >= n_f)
    def _down():
        k = s - n_f

        @pl.when((k == 0) & (e > 0))
        def _drain_previous():
            pltpu.make_async_copy(ybuf, y_dst, ysem).wait()

        def down(w_ref):
            acc = None
            for j in range(n_f):
                part = _dot(act_ref[j], w_ref[0, 0, j * tf:(j + 1) * tf, :].astype(BF16))
                acc = part if acc is None else acc + part
            return acc

        y2 = jnp.concatenate([down(wda_ref), down(wdb_ref)], axis=1)
        _store_slabs(ybuf, (), y2, first_chunk=k * (td // LANES))

        @pl.when(k == n_d2 - 1)
        def _write_back():
            pltpu.make_async_copy(ybuf, y_dst, ysem).start()

        @pl.when((k == n_d2 - 1) & (e == pl.num_programs(0) - 1))
        def _drain_last():
            pltpu.make_async_copy(ybuf, y_dst, ysem).wait()


def _expert_ffn(idx_all, layer, w_gate, w_up, w_down, srcs, seg_rows):
    _, e, d, ff = w_gate.shape
    r = idx_all.shape[-1]
    sr = _slab_rows(d)
    tf = min(256, ff)
    td = min(256, d // 2)
    n_f = ff // tf
    n_d2 = (d // 2) // td
    chunk = min(512, max(seg_rows))
    kern = functools.partial(_ffn_kernel, seg_rows=tuple(seg_rows), chunk=chunk, n_f=n_f, tf=tf, td=td, n_d2=n_d2)
    return pl.pallas_call(
        kern,
        grid=(e, n_f + n_d2),
        in_specs=[
            pl.BlockSpec((1, 1, r), lambda ei, s: (ei, 0, 0), memory_space=pltpu.SMEM),
            pl.BlockSpec((1, 1, d, tf), lambda ei, s: (layer, ei, 0, jnp.minimum(s, n_f - 1))),
            pl.BlockSpec((1, 1, d, tf), lambda ei, s: (layer, ei, 0, jnp.minimum(s, n_f - 1))),
            pl.BlockSpec((1, 1, ff, td), lambda ei, s: (layer, ei, 0, jnp.maximum(s - n_f, 0))),
            pl.BlockSpec((1, 1, ff, td), lambda ei, s: (layer, ei, 0, jnp.maximum(s - n_f, 0) + n_d2)),
        ] + [pl.BlockSpec(memory_space=pl.ANY) for _ in srcs],
        out_specs=pl.BlockSpec(memory_space=pl.ANY),
        out_shape=jax.ShapeDtypeStruct((e * r * sr, LANES), jnp.uint32),
        scratch_shapes=[
            pltpu.VMEM((chunk * sr, LANES), jnp.uint32),
            pltpu.VMEM((r, d), BF16),
            pltpu.VMEM((n_f, r, tf), BF16),
            pltpu.VMEM((r * sr, LANES), jnp.uint32),
            pltpu.SemaphoreType.DMA(()),
            pltpu.SemaphoreType.DMA(()),
        ],
        compiler_params=_cparams(("arbitrary", "arbitrary")),
        name="expert_ffn",
    )(idx_all, w_gate, w_up, w_down, w_down, *srcs)


def _select_kernel(aff_ref, w_ref, idx_ref, bnd_ref, pos_ref, post_ref, *, cap, tile, jchunk):
    n = aff_ref.shape[1]
    bits = pltpu.bitcast(aff_ref[0], jnp.int32)

    def search(it, thr):
        cand = thr | jnp.left_shift(jnp.int32(1), 30 - it)
        cnt = jnp.sum(jnp.where(bits >= cand, 1, 0), axis=0, keepdims=True)
        return jnp.where(cnt >= cap, cand, thr)

    thr = lax.fori_loop(0, 31, search, jnp.zeros((1, LANES), jnp.int32))
    need = (cap - jnp.sum(jnp.where(bits > thr, 1, 0), axis=0, keepdims=True)).astype(F32)

    r = lax.broadcasted_iota(jnp.int32, (tile, tile), 0)
    c = lax.broadcasted_iota(jnp.int32, (tile, tile), 1)
    tri = jnp.where(c <= r, 1.0, 0.0).astype(BF16)

    def chunk(k, carry):
        ceq, cpos = carry
        r0 = pl.multiple_of(k * tile, tile)
        a = aff_ref[0, pl.ds(r0, tile), :]
        bt = pltpu.bitcast(a, jnp.int32)
        eq = bt == thr
        rank = _dot(tri, jnp.where(eq, 1.0, 0.0).astype(BF16)) + ceq
        sel = jnp.where((bt > thr) | (eq & (rank <= need)), 1.0, 0.0)
        pos = _dot(tri, sel.astype(BF16)) + cpos
        pos_ref[pl.ds(r0, tile), :] = pos
        w_ref[0, pl.ds(r0, tile), :] = sel * a
        cpos = pos[tile - 1:tile, :]
        bnd_ref[0, pl.ds(k, 1), :] = cpos.astype(jnp.int32)
        return rank[tile - 1:tile, :], cpos

    zero = jnp.zeros((1, LANES), F32)
    lax.fori_loop(0, n // tile, chunk, (zero, zero))

    for k in range(n // tile):
        post_ref[:, k * tile:(k + 1) * tile] = pos_ref[k * tile:(k + 1) * tile, :].T

    def per_expert(e, carry):
        row = post_ref[pl.ds(e, 1), :]

        def per_j(jc, carry_j):
            j0 = pl.multiple_of(jc * jchunk, jchunk)
            jv = (lax.broadcasted_iota(jnp.int32, (jchunk, n), 0) + j0).astype(F32)
            cnt = jnp.sum(jnp.where(row <= jv, 1.0, 0.0), axis=-1, keepdims=True)
            idx_ref[0, e, pl.ds(j0, jchunk), :] = cnt.astype(jnp.int32)
            return carry_j

        return lax.fori_loop(0, cap // jchunk, per_j, carry)

    lax.fori_loop(0, N_EXPERTS, per_expert, 0)


def _select(aff, cap, tile):
    b, n, _ = aff.shape
    jchunk = min(64, cap)
    kern = functools.partial(_select_kernel, cap=cap, tile=tile, jchunk=jchunk)
    return pl.pallas_call(
        kern,
        grid=(b,),
        in_specs=[pl.BlockSpec((1, n, LANES), lambda bi: (bi, 0, 0))],
        out_specs=[pl.BlockSpec((1, n, LANES), lambda bi: (bi, 0, 0)),
                   pl.BlockSpec((1, N_EXPERTS, cap, 1), lambda bi: (bi, 0, 0, 0)),
                   pl.BlockSpec((1, n // tile, LANES), lambda bi: (bi, 0, 0))],
        out_shape=[jax.ShapeDtypeStruct((b, n, LANES), F32),
                   jax.ShapeDtypeStruct((b, N_EXPERTS, cap, 1), jnp.int32),
                   jax.ShapeDtypeStruct((b, n // tile, LANES), jnp.int32)],
        scratch_shapes=[pltpu.VMEM((n, LANES), F32), pltpu.VMEM((LANES, n), F32)],
        compiler_params=_cparams(("arbitrary",)),
        name="select",
    )(aff)


def _combine_kernel(idx_ref, bnd_ref, x_ref, g2_ref, w_ref, *rest, tile, cap, n_tiles, row0, rows_per_expert,
                    final_norm):
    if final_norm:
        fg_ref, y_hbm, o_ref, buf, cnt_ref, sem = rest
    else:
        y_hbm, o_ref, buf, cnt_ref, sem = rest
    b = pl.program_id(0)
    i = pl.program_id(1)
    plane_rows = N_EXPERTS * tile
    unroll = DMA_ISSUE_UNROLL
    d = x_ref.shape[-1]
    sr = _slab_rows(d)
    wait_unit = max(1, SUBLANES // sr)

    @pl.when((b == 0) & (i == 0))
    def _init():
        buf[...] = jnp.zeros(buf.shape, jnp.uint32)

    @pl.when(i < n_tiles)
    def _issue():
        slot = i % 2
        dst = buf.at[slot]
        dsem = sem.at[slot]
        t0 = i * tile
        total = jnp.int32(0)
        for e in range(N_EXPERTS):
            lo = bnd_ref[0, i, e]
            hi = bnd_ref[0, i + 1, e]
            src0 = e * rows_per_expert + row0 + b * cap
            groups = lax.shift_right_logical(hi - lo + (unroll - 1), unroll.bit_length() - 1)

            def group(k, carry, e=e, lo=lo, hi=hi, src0=src0):
                for u in range(unroll):
                    j = lo + k * unroll + u
                    ok = j < hi
                    jj = jnp.minimum(j, cap - 1)
                    drow = jnp.where(ok, e * tile + idx_ref[0, e, jj] - t0, plane_rows + unroll * e + u)
                    srow = jnp.where(ok, src0 + jj, 0)
                    pltpu.make_async_copy(y_hbm.at[pl.ds(pl.multiple_of(srow * sr, sr), sr), :],
                                          dst.at[pl.ds(pl.multiple_of(drow * sr, sr), sr), :], dsem).start()
                return carry

            lax.fori_loop(0, groups, group, 0)
            total = total + groups * unroll

        pad = (-total) & (wait_unit - 1)

        def issue_pad(k, carry):
            pltpu.make_async_copy(y_hbm.at[pl.ds(pl.multiple_of(k * sr, sr), sr), :],
                                  dst.at[pl.ds(pl.multiple_of((plane_rows + unroll * N_EXPERTS + k) * sr, sr), sr), :],
                                  dsem).start()
            return carry

        lax.fori_loop(0, pad, issue_pad, 0)
        cnt_ref[slot] = (total + pad) * sr

    @pl.when(i > 0)
    def _reduce():
        slot = (i - 1) % 2
        n_rows = pl.multiple_of(cnt_ref[slot], SUBLANES)

        @pl.when(n_rows > 0)
        def _wait():
            pltpu.make_async_copy(y_hbm.at[pl.ds(0, n_rows), :], buf.at[slot, pl.ds(0, n_rows), :],
                                  sem.at[slot]).wait()

        def rows(g, carry):
            r0 = pl.multiple_of(g * COMBINE_ROWS, COMBINE_ROWS)
            wv = w_ref[0, pl.ds(r0, COMBINE_ROWS), :]
            acc_hi = [None] * sr
            acc_lo = [None] * sr
            for e in range(N_EXPERTS):
                wcol = wv[:, e:e + 1]
                for c in range(sr):
                    hi, lo = _unpack_pair(buf[slot, pl.ds((e * tile + r0) * sr + c, COMBINE_ROWS, stride=sr), :])
                    acc_hi[c] = hi * wcol if acc_hi[c] is None else acc_hi[c] + hi * wcol
                    acc_lo[c] = lo * wcol if acc_lo[c] is None else acc_lo[c] + lo * wcol
            acc = jnp.concatenate(acc_hi + acc_lo, axis=1)
            out = x_ref[0, pl.ds(r0, COMBINE_ROWS), :] + g2_ref[0] * acc
            if final_norm:
                out = _rms(out, fg_ref[...])
            o_ref[0, pl.ds(r0, COMBINE_ROWS), :] = out
            return carry

        lax.fori_loop(0, tile // COMBINE_ROWS, rows, 0)


def _combine(idx, bnd, x, g2, w, y_flat, tile, row0, rows_per_expert, final_g=None):
    b, n, d = x.shape
    cap = idx.shape[-1]
    n_tiles = n // tile
    final_norm = final_g is not None
    kern = functools.partial(_combine_kernel, tile=tile, cap=cap, n_tiles=n_tiles, row0=row0,
                             rows_per_expert=rows_per_expert, final_norm=final_norm)
    row = lambda bi, i: (bi, jnp.maximum(i - 1, 0), 0)
    per_b = lambda bi, i: (bi, 0, 0)
    spare = DMA_ISSUE_UNROLL * N_EXPERTS + SUBLANES
    in_specs = [
        pl.BlockSpec((1, N_EXPERTS, cap), per_b, memory_space=pltpu.SMEM),
        pl.BlockSpec((1, n // tile + 1, N_EXPERTS), per_b, memory_space=pltpu.SMEM),
        pl.BlockSpec((1, tile, d), row),
        pl.BlockSpec((1, 1, d), per_b),
        pl.BlockSpec((1, tile, LANES), row),
    ]
    args = [idx, bnd, x, g2, w]
    if final_norm:
        in_specs.append(pl.BlockSpec((1, d), lambda bi, i: (0, 0)))
        args.append(final_g)
    in_specs.append(pl.BlockSpec(memory_space=pl.ANY))
    args.append(y_flat)
    return pl.pallas_call(
        kern,
        grid=(b, n_tiles + 1),
        in_specs=in_specs,
        out_specs=pl.BlockSpec((1, tile, d), row),
        out_shape=jax.ShapeDtypeStruct((b, n, d), F32),
        scratch_shapes=[pltpu.VMEM((2, (N_EXPERTS * tile + spare) * _slab_rows(d), LANES), jnp.uint32),
                        pltpu.SMEM((2,), jnp.int32), pltpu.SemaphoreType.DMA((2,))],
        compiler_params=_cparams(("arbitrary", "arbitrary")),
        name="combine",
    )(*args)


def _route(aff, cap, tile, row_stride):
    b = aff.shape[0]
    w, idx, bnd = _select(aff, cap, tile)
    idx = idx.reshape(b, N_EXPERTS, cap)
    bnd = jnp.pad(bnd[:, :, :N_EXPERTS], ((0, 0), (1, 0), (0, 0)))
    rows = idx + (jnp.arange(b, dtype=jnp.int32) * row_stride)[:, None, None]
    rows = jnp.swapaxes(rows, 0, 1).reshape(N_EXPERTS, b * cap)
    return w, idx, bnd, rows


def _axial_tables(n_tok, rot_dim):
    rows = (jnp.arange(n_tok, dtype=jnp.int32) // GRID_W).astype(F32)
    cols = (jnp.arange(n_tok, dtype=jnp.int32) % GRID_W).astype(F32)
    n_freq = rot_dim // 4
    inv_freq = ROPE_THETA ** (-jnp.arange(n_freq, dtype=F32) / n_freq)
    ang = jnp.concatenate([rows[:, None] * inv_freq, cols[:, None] * inv_freq], axis=-1)
    return jnp.cos(ang), jnp.sin(ang)


def _rope_tables(n_tok, identity):
    if identity:
        one = jnp.ones((n_tok, LANES), F32)
        zero = jnp.zeros((n_tok, LANES), F32)
        return one, zero, one, zero
    cos_s, sin_s = _axial_tables(n_tok, SWA_HEAD_DIM)
    css = jnp.tile(jnp.concatenate([cos_s, cos_s], axis=-1), (1, 2))
    sns = jnp.tile(jnp.concatenate([-sin_s, sin_s], axis=-1), (1, 2))
    cos_m, sin_m = _axial_tables(n_tok, MLA_ROPE)
    ones = jnp.ones((n_tok, MLA_NOPE), F32)
    tail = LANES - MLA_NOPE - MLA_ROPE
    csm = jnp.concatenate([ones, cos_m, cos_m, jnp.ones((n_tok, tail), F32)], axis=-1)
    snm = jnp.concatenate([0.0 * ones, -sin_m, sin_m, jnp.zeros((n_tok, tail), F32)], axis=-1)
    return css, sns, csm, snm


def _mla_weight_slots(w_uq, w_ukv):
    dq = MLA_NOPE + MLA_ROPE
    wuq = jnp.pad(w_uq.reshape(MLA_Q_LORA, MLA_HEADS, dq), ((0, 0), (0, 0), (0, LANES - dq)))
    wkv = w_ukv.reshape(MLA_KV_LORA, MLA_HEADS, MLA_NOPE + MLA_V)
    wuk = jnp.pad(wkv[..., :MLA_NOPE], ((0, 0), (0, 0), (0, LANES - MLA_NOPE)))
    wv = wkv[..., MLA_NOPE:]
    even = jnp.pad(wv, ((0, 0), (0, 0), (0, LANES - MLA_V)))
    odd = jnp.pad(wv, ((0, 0), (0, 0), (LANES - MLA_V, 0)))
    is_even = (jnp.arange(MLA_HEADS) % 2 == 0)[None, :, None]
    wuv = jnp.where(is_even, even, odd)
    flat = lambda w: w.reshape(w.shape[0], MLA_HEADS * LANES).astype(BF16)
    return flat(wuq), flat(wuk), flat(wuv)


def _phase_tables(k, m, period):
    ang = ((k[:, None] * m[None, :]) % period).astype(F32) * (2.0 * math.pi / period)
    return jnp.cos(ang), jnp.sin(ang)


def _dft_tables(n_tok):
    k = jnp.arange(n_tok, dtype=jnp.int32)
    side = math.isqrt(n_tok)
    if side * side != n_tok:
        c, s = _phase_tables(k, k, n_tok)
        return c.astype(BF16), s.astype(BF16)
    m = jnp.arange(side, dtype=jnp.int32)
    ca, sa = _phase_tables(k, m * side, n_tok)
    cb, sb = _phase_tables(k, m, n_tok)
    c = ca[:, :, None] * cb[:, None, :] - sa[:, :, None] * sb[:, None, :]
    s = sa[:, :, None] * cb[:, None, :] + ca[:, :, None] * sb[:, None, :]
    return c.reshape(n_tok, n_tok).astype(BF16), s.reshape(n_tok, n_tok).astype(BF16)


def _channel_dft(n_tok):
    ch = jnp.arange(GROUP_W, dtype=jnp.int32)
    c, s = _phase_tables(ch % FNET_HEAD_DIM, ch % FNET_HEAD_DIM, FNET_HEAD_DIM)
    same = (ch[:, None] // FNET_HEAD_DIM) == (ch[None, :] // FNET_HEAD_DIM)
    scale = (n_tok * FNET_HEAD_DIM) ** -0.5
    return (jnp.concatenate([jnp.where(same, c, 0.0), jnp.where(same, -s, 0.0)], axis=1) * scale).astype(BF16)


def kernel(x, c, ctx, c_ctx, ada_w, ada_b, norm1_g, norm2_g, w_in, conv_w, swa_sink, mla_q_norm_g, mla_w_uq,
           mla_kv_norm_g, mla_w_ukv, out_norm_g, w_out, w_router, w_gate, w_up, w_down, final_norm_g):
    b, n, d = x.shape
    l = ctx.shape[1]
    depth = ada_w.shape[0]
    cap_x = EC_CAPACITY * n // N_EXPERTS
    cap_c = EC_CAPACITY * l // N_EXPERTS
    tile_x = min(128, n)
    tile_c = min(128, l)

    c_rows = jnp.concatenate([c, c_ctx[None, :], jnp.zeros((-(b + 1) % 8, d), F32)], axis=0)
    mods = _adaln(c_rows, ada_w, ada_b)
    rope_x = _rope_tables(n, identity=False)
    rope_c = _rope_tables(l, identity=True)
    dft_x, dft_c = _dft_tables(n), _dft_tables(l)
    cdft_x, cdft_c = _channel_dft(n), _channel_dft(l)

    for layer in range(depth):
        last = layer == depth - 1
        mx = mods[layer, :b].reshape(b, 1, 6, d)
        mc = jnp.broadcast_to(mods[layer, b].reshape(1, 1, 6, d), (b, 1, 6, d))
        sh1, sc1, g1, sh2, sc2, g2 = [mx[:, :, k] for k in range(6)]
        csh1, csc1, cg1, csh2, csc2, cg2 = [mc[:, :, k] for k in range(6)]

        w_in_p = jnp.pad(w_in[layer].astype(BF16), ((0, 0), (0, IN_COLS_PAD - IN_COLS)))
        wuq, wuk, wuv = _mla_weight_slots(mla_w_uq[layer], mla_w_ukv[layer])
        n1 = norm1_g[layer][None, :]
        qg = mla_q_norm_g[layer][None, :]
        kvg = mla_kv_norm_g[layer][None, :]
        px = _mixer_inputs(x, sh1, sc1, n1, w_in_p, qg, kvg, wuq, wuk, wuv, rope_x, cdft_x)
        pc = _mixer_inputs(ctx, csh1, csc1, n1, w_in_p, qg, kvg, wuq, wuk, wuv, rope_c, cdft_c)
        cb, u, sq, skp, svp, fu, mq, mk, mvp = px
        ccb, cu, csq, cskp, csvp, cfu, cmq, cmk, cmvp = pc

        sink = swa_sink[layer]
        og = out_norm_g[layer][None, :]
        wo = w_out[layer].astype(BF16)
        n2 = norm2_g[layer][None, :]
        wr_hi = w_router[layer].astype(BF16)
        wr_lo = (w_router[layer] - wr_hi.astype(F32)).astype(BF16)
        wr = jnp.pad(jnp.concatenate([wr_hi, wr_lo], axis=1), ((0, 0), (0, LANES - 2 * N_EXPERTS)))

        yx = (_swa(sink, sq, cskp, csvp, skp, svp), _fourier(fu, *dft_x), _mla(mq, cmk, cmvp, mk, mvp))
        x, fx, aff = _mixer_out(x, g1, cb, u, conv_w[layer], *yx, og, wo, n2, sh2, sc2, wr)
        srcs = [fx.reshape(-1, LANES)]
        seg_rows = [b * cap_x]
        wx, idx_x, bnd_x, rows_x = _route(aff, cap_x, tile_x, n)
        row_parts = [rows_x]
        if not last:
            yc = (_swa(sink, csq, cskp, csvp), _fourier(cfu, *dft_c), _mla(cmq, cmk, cmvp))
            ctx, fc, affc = _mixer_out(ctx, cg1, ccb, cu, conv_w[layer], *yc, og, wo, n2, csh2, csc2, wr)
            srcs.append(fc.reshape(-1, LANES))
            seg_rows.append(b * cap_c)
            wc, idx_c, bnd_c, rows_c = _route(affc, cap_c, tile_c, l)
            row_parts.append(rows_c)
        idx_all = jnp.concatenate(row_parts, axis=1)[:, None, :]
        y_flat = _expert_ffn(idx_all, layer, w_gate, w_up, w_down, srcs, seg_rows)
        rows_per_expert = idx_all.shape[-1]
        x = _combine(idx_x, bnd_x, x, g2, wx, y_flat, tile_x, 0, rows_per_expert,
                     final_g=final_norm_g[None, :] if last else None)
        if not last:
            ctx = _combine(idx_c, bnd_c, ctx, cg2, wc, y_flat, tile_c, b * cap_x, rows_per_expert)
    return x
```

```python
import functools
import math

import jax
import jax.numpy as jnp
from jax import lax
from jax.experimental import pallas as pl
from jax.experimental.pallas import tpu as pltpu

F32 = jnp.float32
BF16 = jnp.bfloat16
HIGHEST = lax.Precision.HIGHEST

GRID_W = 64
GROUP_W = 512
N_GROUPS = 4
SWA_HEAD_DIM = 64
SWA_HEADS = 8
SWA_KV_HEADS = 2
SWA_BLOCK = 128
FNET_HEAD_DIM = 64
MLA_HEADS = 8
MLA_NOPE = 64
MLA_ROPE = 32
MLA_V = 64
MLA_Q_LORA = 512
MLA_KV_LORA = 256
N_EXPERTS = 16
EC_CAPACITY = 2
ROPE_THETA = 10000.0
EPS = 1e-6
IN_COLS = 3616
IN_COLS_PAD = 3712
LANES = 128
SUBLANES = 8
DMA_ISSUE_UNROLL = 4
COMBINE_ROWS = 8
FFN_UNPACK_ROWS = 512
CONV_HALO = 16
NEG = -1e30
LOG2E = 1.4426950408889634

VMEM_LIMIT = 60 * 1024 * 1024


def _cparams(sem):
    return pltpu.CompilerParams(dimension_semantics=sem, vmem_limit_bytes=VMEM_LIMIT)


def _dot(a, b):
    return jnp.dot(a, b, preferred_element_type=F32)


def _dot_t(a, b):
    return lax.dot_general(a, b, (((1,), (1,)), ((), ())), preferred_element_type=F32)


def _rms(x, g):
    return x * lax.rsqrt(jnp.mean(x * x, axis=-1, keepdims=True) + EPS) * g


def _pack_pair(hi, lo):
    hb = lax.bitcast_convert_type(hi.astype(BF16).astype(F32), jnp.uint32)
    lb = lax.bitcast_convert_type(lo.astype(BF16).astype(F32), jnp.uint32)
    return hb | (lb >> 16)


def _unpack_pair(v):
    hi = lax.bitcast_convert_type(v & jnp.uint32(0xFFFF0000), F32)
    lo = lax.bitcast_convert_type(v << 16, F32)
    return hi, lo


def _adaln_kernel(c_ref, w_ref, b_ref, o_ref):
    c = c_ref[...]
    s = c / (1.0 + jnp.exp(-c))
    o_ref[0] = lax.dot_general(s, w_ref[0], (((1,), (0,)), ((), ())),
                               precision=HIGHEST, preferred_element_type=F32) + b_ref[0]


def _adaln(c_rows, ada_w, ada_b):
    depth, d, d6 = ada_w.shape
    tn = 1024 if d6 % 1024 == 0 else d6
    rows = c_rows.shape[0]
    return pl.pallas_call(
        _adaln_kernel,
        grid=(depth, d6 // tn),
        in_specs=[
            pl.BlockSpec((rows, d), lambda l, j: (0, 0)),
            pl.BlockSpec((1, d, tn), lambda l, j: (l, 0, j)),
            pl.BlockSpec((1, 1, tn), lambda l, j: (l, 0, j)),
        ],
        out_specs=pl.BlockSpec((1, rows, tn), lambda l, j: (l, 0, j)),
        out_shape=jax.ShapeDtypeStruct((depth, rows, d6), F32),
        compiler_params=_cparams(("arbitrary", "arbitrary")),
        name="adaln",
    )(c_rows, ada_w, ada_b.reshape(depth, 1, d6))


def _swap_halves(x, width):
    lane = lax.broadcasted_iota(jnp.int32, x.shape, 1)
    first = (lane % (2 * width)) < width
    return jnp.where(first, pltpu.roll(x, LANES - width, 1), pltpu.roll(x, width, 1))


def _mixin_kernel(x_ref, sh_ref, sc_ref, g_ref, w_ref, qg_ref, kvg_ref, wuq_ref, wuk_ref, wuv_ref,
                  css_ref, sns_ref, csm_ref, snm_ref, cdft_ref,
                  cb_ref, u_ref, sq_ref, skp_ref, svp_ref, fu_ref, mq_ref, mk_ref, mvp_ref):
    x = x_ref[0]
    h = _rms(x, g_ref[...]) * (1.0 + sc_ref[0]) + sh_ref[0]
    hb = h.astype(BF16)

    pc = _dot(hb, w_ref[:, 0:3 * GROUP_W])
    cb_ref[0] = pc[:, 0:GROUP_W].astype(BF16)
    u_ref[0] = (pc[:, GROUP_W:2 * GROUP_W] * pc[:, 2 * GROUP_W:3 * GROUP_W]).astype(BF16)

    c0 = 3 * GROUP_W
    ps = _dot(hb, w_ref[:, c0:c0 + 768])
    css = css_ref[...]
    sns = sns_ref[...]
    qscale = SWA_HEAD_DIM ** -0.5 * LOG2E
    for c in range(4):
        t = ps[:, c * LANES:(c + 1) * LANES]
        t = (t * css + _swap_halves(t, SWA_HEAD_DIM // 2) * sns) * qscale
        sq_ref[0, :, c * LANES:(c + 1) * LANES] = t.astype(BF16)
    sk = ps[:, 512:640]
    sk = sk * css + _swap_halves(sk, SWA_HEAD_DIM // 2) * sns
    sv = ps[:, 640:768]
    lane = lax.broadcasted_iota(jnp.int32, sk.shape, 1)
    lo = lane < SWA_HEAD_DIM
    for src, dst, fill in ((sk, skp_ref, 0.0), (sv, svp_ref, 1.0)):
        sw = pltpu.roll(src, SWA_HEAD_DIM, 1)
        pad_hi = jnp.where(lane == SWA_HEAD_DIM, fill, 0.0)
        pad_lo = jnp.where(lane == 0, fill, 0.0)
        dst[0, :, 0:128] = jnp.where(lo, src, pad_hi).astype(BF16)
        dst[0, :, 128:256] = jnp.where(lo, pad_lo, sw).astype(BF16)
        dst[0, :, 256:384] = jnp.where(lo, sw, pad_hi).astype(BF16)
        dst[0, :, 384:512] = jnp.where(lo, pad_lo, src).astype(BF16)

    c0 += 768
    fu = _dot(hb, w_ref[:, c0:c0 + GROUP_W]).astype(BF16)
    fu_ref[0] = _dot(fu, cdft_ref[...]).astype(BF16)

    c0 += GROUP_W
    pm = _dot(hb, w_ref[:, c0:c0 + 896])
    qn = _rms(pm[:, 0:MLA_Q_LORA], qg_ref[...]).astype(BF16)
    kvn = _rms(pm[:, MLA_Q_LORA:MLA_Q_LORA + MLA_KV_LORA], kvg_ref[...]).astype(BF16)
    csm = csm_ref[...]
    snm = snm_ref[...]

    def rope_m(t):
        lane_m = lax.broadcasted_iota(jnp.int32, t.shape, 1)
        sw = jnp.where(lane_m < MLA_NOPE + MLA_ROPE // 2,
                       pltpu.roll(t, LANES - MLA_ROPE // 2, 1), pltpu.roll(t, MLA_ROPE // 2, 1))
        return t * csm + sw * snm

    q = _dot(qn, wuq_ref[...])
    mscale = (MLA_NOPE + MLA_ROPE) ** -0.5 * LOG2E
    kpe = rope_m(pltpu.roll(pm[:, 768:896], MLA_NOPE, 1))
    kn = _dot(kvn, wuk_ref[...])
    mv = _dot(kvn, wuv_ref[...])
    lane_v = lax.broadcasted_iota(jnp.int32, kpe.shape, 1)
    for hd in range(MLA_HEADS):
        sl = slice(hd * LANES, (hd + 1) * LANES)
        mq_ref[0, :, sl] = (rope_m(q[:, sl]) * mscale).astype(BF16)
        mk_ref[0, :, sl] = (kn[:, sl] + kpe).astype(BF16)
        ones_lane = MLA_V if hd % 2 == 0 else 0
        mvp_ref[0, :, sl] = jnp.where(lane_v == ones_lane, 1.0, mv[:, sl]).astype(BF16)


def _mixer_inputs(x, shift, scale, norm_g, w_in_p, qg, kvg, wuq, wuk, wuv, rope_tabs, cdft):
    b, n, d = x.shape
    tm = min(256, n)
    css, sns, csm, snm = rope_tabs
    row = lambda bi, i: (bi, i, 0)
    const2 = lambda bi, i: (0, 0)
    tab = lambda bi, i: (i, 0)
    outs = [jax.ShapeDtypeStruct((b, n, w), BF16) for w in (512, 512, 512, 512, 512, 1024, 1024, 1024, 1024)]
    return pl.pallas_call(
        _mixin_kernel,
        grid=(b, n // tm),
        in_specs=[
            pl.BlockSpec((1, tm, d), row),
            pl.BlockSpec((1, 1, d), lambda bi, i: (bi, 0, 0)),
            pl.BlockSpec((1, 1, d), lambda bi, i: (bi, 0, 0)),
            pl.BlockSpec((1, d), const2),
            pl.BlockSpec((d, IN_COLS_PAD), const2),
            pl.BlockSpec((1, MLA_Q_LORA), const2),
            pl.BlockSpec((1, MLA_KV_LORA), const2),
            pl.BlockSpec((MLA_Q_LORA, 1024), const2),
            pl.BlockSpec((MLA_KV_LORA, 1024), const2),
            pl.BlockSpec((MLA_KV_LORA, 1024), const2),
            pl.BlockSpec((tm, LANES), tab),
            pl.BlockSpec((tm, LANES), tab),
            pl.BlockSpec((tm, LANES), tab),
            pl.BlockSpec((tm, LANES), tab),
            pl.BlockSpec((GROUP_W, 2 * GROUP_W), const2),
        ],
        out_specs=[pl.BlockSpec((1, tm, o.shape[-1]), row) for o in outs],
        out_shape=outs,
        compiler_params=_cparams(("arbitrary", "arbitrary")),
        name="mixer_inputs",
    )(x, shift, scale, norm_g, w_in_p, qg, kvg, wuq, wuk, wuv, css, sns, csm, snm, cdft)


def _fourier_kernel(ct_ref, st_ref, ab_ref, o_ref):
    acc = _dot(ct_ref[...], ab_ref[0, :, 0:GROUP_W]) + _dot(st_ref[...], ab_ref[0, :, GROUP_W:2 * GROUP_W])
    o_ref[0] = acc.astype(BF16)


def _fourier(ab, ctab, stab):
    b, n, _ = ab.shape
    tk = min(512, n)
    return pl.pallas_call(
        _fourier_kernel,
        grid=(b, n // tk),
        in_specs=[
            pl.BlockSpec((tk, n), lambda bi, i: (i, 0)),
            pl.BlockSpec((tk, n), lambda bi, i: (i, 0)),
            pl.BlockSpec((1, n, 2 * GROUP_W), lambda bi, i: (bi, 0, 0)),
        ],
        out_specs=pl.BlockSpec((1, tk, GROUP_W), lambda bi, i: (bi, i, 0)),
        out_shape=jax.ShapeDtypeStruct((b, n, GROUP_W), BF16),
        compiler_params=_cparams(("arbitrary", "arbitrary")),
        name="fourier",
    )(ctab, stab, ab)


def _swa_kernel(sink_ref, q_ref, kc_ref, vc_ref, *rest, has_band, n_blocks):
    if has_band:
        kp_ref, vp_ref, o_ref = rest
    else:
        (o_ref,) = rest
    i = pl.program_id(1)
    t = SWA_BLOCK
    if has_band:
        start = pl.multiple_of(jnp.clip(i - 1, 0, n_blocks - 3) * t, t)
        qpos = i * t + lax.broadcasted_iota(jnp.int32, (2 * t, 3 * t), 0) % t
        kpos = start + lax.broadcasted_iota(jnp.int32, (2 * t, 3 * t), 1)
        valid = jnp.abs(kpos - qpos) <= SWA_BLOCK
    first = lax.broadcasted_iota(jnp.int32, (2 * t, 1), 0) < t
    lane = lax.broadcasted_iota(jnp.int32, (2 * t, LANES), 1)
    acc = [None] * 4
    for g in range(SWA_KV_HEADS):
        for hh in range(2):
            slot = 2 * g + hh
            sl = slice(slot * LANES, (slot + 1) * LANES)
            cols = (2 * g, 2 * g + 1)
            q2 = jnp.concatenate([q_ref[0, :, c * LANES:(c + 1) * LANES] for c in cols], axis=0)
            sink = jnp.where(first, sink_ref[2 * cols[0] + hh], sink_ref[2 * cols[1] + hh]) * LOG2E
            s_ctx = _dot_t(q2, kc_ref[0, :, sl])
            m = jnp.maximum(jnp.max(s_ctx, axis=-1, keepdims=True), sink)
            if has_band:
                s_b = jnp.where(valid, _dot_t(q2, kp_ref[0, pl.ds(start, 3 * t), sl]), NEG)
                m = jnp.maximum(m, jnp.max(s_b, axis=-1, keepdims=True))
            o = _dot(jnp.exp2(s_ctx - m).astype(BF16), vc_ref[0, :, sl])
            if has_band:
                o = o + _dot(jnp.exp2(s_b - m).astype(BF16), vp_ref[0, pl.ds(start, 3 * t), sl])
            ones_lane = SWA_HEAD_DIM if hh == 0 else 0
            den = o[:, ones_lane:ones_lane + 1] + jnp.exp2(sink - m)
            keep = (lane < SWA_HEAD_DIM) if hh == 0 else (lane >= SWA_HEAD_DIM)
            o = jnp.where(keep, o / den, 0.0)
            for k, c in enumerate(cols):
                part = o[k * t:(k + 1) * t]
                acc[c] = part if acc[c] is None else acc[c] + part
    for c in range(4):
        o_ref[0, :, c * LANES:(c + 1) * LANES] = acc[c].astype(BF16)


def _swa(sink, q, kc, vc, kp=None, vp=None):
    b, n, _ = q.shape
    l = kc.shape[1]
    t = SWA_BLOCK
    has_band = kp is not None
    blk = lambda bi, i: (bi, i, 0)
    whole = lambda bi, i: (bi, 0, 0)
    in_specs = [
        pl.BlockSpec(memory_space=pltpu.SMEM),
        pl.BlockSpec((1, t, 512), blk),
        pl.BlockSpec((1, l, 512), whole),
        pl.BlockSpec((1, l, 512), whole),
    ]
    args = [sink, q, kc, vc]
    if has_band:
        in_specs += [pl.BlockSpec((1, n, 512), whole), pl.BlockSpec((1, n, 512), whole)]
        args += [kp, vp]
    return pl.pallas_call(
        functools.partial(_swa_kernel, has_band=has_band, n_blocks=n // t),
        grid=(b, n // t),
        in_specs=in_specs,
        out_specs=pl.BlockSpec((1, t, 512), blk),
        out_shape=jax.ShapeDtypeStruct((b, n, 512), BF16),
        compiler_params=_cparams(("arbitrary", "arbitrary")),
        name="swa_latent" if has_band else "swa_context",
    )(*args)


def _mla_kernel(q_ref, kc_ref, vc_ref, *rest, has_lat):
    if has_lat:
        k_ref, v_ref, o_ref = rest
    else:
        (o_ref,) = rest
    lane = lax.broadcasted_iota(jnp.int32, (q_ref.shape[1], LANES), 1)
    out = None
    for hh in range(2):
        sl = slice(hh * LANES, (hh + 1) * LANES)
        q = q_ref[0, :, sl]
        s_ctx = _dot_t(q, kc_ref[0, :, sl])
        m = jnp.max(s_ctx, axis=-1, keepdims=True)
        if has_lat:
            s_lat = _dot_t(q, k_ref[0, :, sl])
            m = jnp.maximum(m, jnp.max(s_lat, axis=-1, keepdims=True))
        o = _dot(jnp.exp2(s_ctx - m).astype(BF16), vc_ref[0, :, sl])
        if has_lat:
            o = o + _dot(jnp.exp2(s_lat - m).astype(BF16), v_ref[0, :, sl])
        ones_lane = MLA_V if hh == 0 else 0
        keep = (lane < MLA_V) if hh == 0 else (lane >= MLA_V)
        o = jnp.where(keep, o / o[:, ones_lane:ones_lane + 1], 0.0)
        out = o if out is None else out + o
    o_ref[0] = out.astype(BF16)


def _mla(q, kc, vc, k=None, v=None):
    b, n, _ = q.shape
    l = kc.shape[1]
    tq = min(256, n)
    has_lat = k is not None
    pair = 2 * LANES
    in_specs = [
        pl.BlockSpec((1, tq, pair), lambda bi, h, i: (bi, i, h)),
        pl.BlockSpec((1, l, pair), lambda bi, h, i: (bi, 0, h)),
        pl.BlockSpec((1, l, pair), lambda bi, h, i: (bi, 0, h)),
    ]
    args = [q, kc, vc]
    if has_lat:
        in_specs += [pl.BlockSpec((1, n, pair), lambda bi, h, i: (bi, 0, h)),
                     pl.BlockSpec((1, n, pair), lambda bi, h, i: (bi, 0, h))]
        args += [k, v]
    return pl.pallas_call(
        functools.partial(_mla_kernel, has_lat=has_lat),
        grid=(b, MLA_HEADS // 2, n // tq),
        in_specs=in_specs,
        out_specs=pl.BlockSpec((1, tq, LANES), lambda bi, h, i: (bi, i, h)),
        out_shape=jax.ShapeDtypeStruct((b, n, MLA_HEADS // 2 * LANES), BF16),
        compiler_params=_cparams(("arbitrary", "arbitrary", "arbitrary")),
        name="mla_latent" if has_lat else "mla_context",
    )(*args)


def _mixout_kernel(x_ref, g1_ref, cb_ref, u_ref, up_ref, un_ref, cw_ref, ys_ref, yf_ref, ym_ref, og_ref, wo_ref,
                   n2_ref, sh2_ref, sc2_ref, wr_ref, xo_ref, fx_ref, aff_ref):
    i = pl.program_id(1)
    u = u_ref[0].astype(F32)
    tm = u.shape[0]
    prev = jnp.where(i > 0, up_ref[0].astype(F32)[CONV_HALO - 1:CONV_HALO, :], 0.0)
    nxt = jnp.where(i < pl.num_programs(1) - 1, un_ref[0].astype(F32)[0:1, :], 0.0)
    row = lax.broadcasted_iota(jnp.int32, u.shape, 0)
    u_before = jnp.where(row == 0, prev, pltpu.roll(u, 1, 0))
    u_after = jnp.where(row == tm - 1, nxt, pltpu.roll(u, tm - 1, 0))
    conv = cb_ref[0].astype(F32) * (cw_ref[0:1, :] * u_before + cw_ref[1:2, :] * u + cw_ref[2:3, :] * u_after)
    groups = (conv, ys_ref[0].astype(F32), yf_ref[0].astype(F32), ym_ref[0].astype(F32))
    acc = None
    for gi, y in enumerate(groups):
        sl = slice(gi * GROUP_W, (gi + 1) * GROUP_W)
        yn = _rms(y, og_ref[:, sl]).astype(BF16)
        part = _dot(yn, wo_ref[sl, :])
        acc = part if acc is None else acc + part
    xn = x_ref[0] + g1_ref[0] * acc
    xo_ref[0] = xn
    fx = _rms(xn, n2_ref[...]) * (1.0 + sc2_ref[0]) + sh2_ref[0]
    half = fx.shape[1] // 2
    fx_ref[0] = _pack_pair(fx[:, :half], fx[:, half:])
    fh = fx.astype(BF16)
    fl = (fx - fh.astype(F32)).astype(BF16)
    parts = _dot(fh, wr_ref[...]) + _dot(fl, wr_ref[...])
    logits = parts + pltpu.roll(parts, LANES - N_EXPERTS, 1)
    lane = lax.broadcasted_iota(jnp.int32, logits.shape, 1)
    logits = jnp.where(lane < N_EXPERTS, logits, NEG)
    e = jnp.exp(logits - jnp.max(logits, axis=-1, keepdims=True))
    aff_ref[0] = e / jnp.sum(e, axis=-1, keepdims=True)


def _mixer_out(x, g1, cb, u, conv_w, ys, yf, ym, out_norm_g, w_out_b, norm2_g, sh2, sc2, w_router_p):
    b, n, d = x.shape
    tm = min(256, n)
    halo_per_tile = tm // CONV_HALO
    n_halo = n // CONV_HALO
    row = lambda bi, i: (bi, i, 0)
    per_b = lambda bi, i: (bi, 0, 0)
    const2 = lambda bi, i: (0, 0)
    return pl.pallas_call(
        _mixout_kernel,
        grid=(b, n // tm),
        in_specs=[
            pl.BlockSpec((1, tm, d), row),
            pl.BlockSpec((1, 1, d), per_b),
            pl.BlockSpec((1, tm, GROUP_W), row),
            pl.BlockSpec((1, tm, GROUP_W), row),
            pl.BlockSpec((1, CONV_HALO, GROUP_W), lambda bi, i: (bi, jnp.maximum(i * halo_per_tile - 1, 0), 0)),
            pl.BlockSpec((1, CONV_HALO, GROUP_W),
                         lambda bi, i: (bi, jnp.minimum((i + 1) * halo_per_tile, n_halo - 1), 0)),
            pl.BlockSpec((3, GROUP_W), const2),
            pl.BlockSpec((1, tm, GROUP_W), row),
            pl.BlockSpec((1, tm, GROUP_W), row),
            pl.BlockSpec((1, tm, GROUP_W), row),
            pl.BlockSpec((1, N_GROUPS * GROUP_W), const2),
            pl.BlockSpec((N_GROUPS * GROUP_W, d), const2),
            pl.BlockSpec((1, d), const2),
            pl.BlockSpec((1, 1, d), per_b),
            pl.BlockSpec((1, 1, d), per_b),
            pl.BlockSpec((d, LANES), const2),
        ],
        out_specs=[pl.BlockSpec((1, tm, d), row), pl.BlockSpec((1, tm, d // 2), row),
                   pl.BlockSpec((1, tm, LANES), row)],
        out_shape=[jax.ShapeDtypeStruct((b, n, d), F32), jax.ShapeDtypeStruct((b, n, d // 2), jnp.uint32),
                   jax.ShapeDtypeStruct((b, n, LANES), F32)],
        compiler_params=_cparams(("arbitrary", "arbitrary")),
        name="mixer_out",
    )(x, g1, cb, u, u, u, conv_w, ys, yf, ym, out_norm_g, w_out_b, norm2_g, sh2, sc2, w_router_p)


def _ffn_kernel(idx_ref, idx_next_ref, wg_ref, wu_ref, wda_ref, wdb_ref, *rest, seg_rows, n_f, tf):
    n_src = len(seg_rows)
    srcs = rest[:n_src]
    y_ref = rest[n_src]
    stage_ref, xs_ref, act_ref, sem = rest[n_src + 1:]
    e = pl.program_id(0)
    s = pl.program_id(1)
    r, d = xs_ref.shape
    half = d // 2

    def start_gather(ids_ref):
        base = 0
        for src, rows in zip(srcs, seg_rows):

            def issue(j, carry, base=base, src=src):
                pltpu.make_async_copy(src.at[pl.ds(ids_ref[0, 0, base + j], 1), :],
                                      stage_ref.at[pl.ds(base + j, 1), :], sem).start()
                return carry

            lax.fori_loop(0, rows, issue, 0, unroll=DMA_ISSUE_UNROLL)
            base += rows

    @pl.when(s == 0)
    def _gather():
        @pl.when(e == 0)
        def _first():
            start_gather(idx_ref)

        pltpu.make_async_copy(srcs[0].at[pl.ds(0, r), :], stage_ref, sem).wait()
        for r0 in range(0, r, FFN_UNPACK_ROWS):
            rn = min(FFN_UNPACK_ROWS, r - r0)
            hi, lo = _unpack_pair(stage_ref[r0:r0 + rn, :])
            xs_ref[r0:r0 + rn, 0:half] = hi.astype(BF16)
            xs_ref[r0:r0 + rn, half:d] = lo.astype(BF16)

        @pl.when(e + 1 < pl.num_programs(0))
        def _prefetch():
            start_gather(idx_next_ref)

    @pl.when(s < n_f)
    def _gate_up():
        xs = xs_ref[...]
        a = _dot(xs, wg_ref[0, 0].astype(BF16))
        u = _dot(xs, wu_ref[0, 0].astype(BF16))
        act_ref[s] = (a / (1.0 + jnp.exp(-a)) * u).astype(BF16)

    @pl.when(s >= n_f)
    def _down():
        def down(w_ref):
            acc = None
            for j in range(n_f):
                part = _dot(act_ref[j], w_ref[0, 0, j * tf:(j + 1) * tf, :].astype(BF16))
                acc = part if acc is None else acc + part
            return acc

        y_ref[0] = _pack_pair(down(wda_ref), down(wdb_ref))


def _expert_ffn(idx_all, layer, w_gate, w_up, w_down, srcs, seg_rows):
    _, e, d, ff = w_gate.shape
    r = idx_all.shape[-1]
    tf = min(256, ff)
    td = min(256, d // 2)
    n_f = ff // tf
    n_d2 = (d // 2) // td
    kern = functools.partial(_ffn_kernel, seg_rows=tuple(seg_rows), n_f=n_f, tf=tf)
    return pl.pallas_call(
        kern,
        grid=(e, n_f + n_d2),
        in_specs=[
            pl.BlockSpec((1, 1, r), lambda ei, s: (ei, 0, 0), memory_space=pltpu.SMEM),
            pl.BlockSpec((1, 1, r), lambda ei, s: (jnp.minimum(ei + 1, e - 1), 0, 0), memory_space=pltpu.SMEM),
            pl.BlockSpec((1, 1, d, tf), lambda ei, s: (layer, ei, 0, jnp.minimum(s, n_f - 1))),
            pl.BlockSpec((1, 1, d, tf), lambda ei, s: (layer, ei, 0, jnp.minimum(s, n_f - 1))),
            pl.BlockSpec((1, 1, ff, td), lambda ei, s: (layer, ei, 0, jnp.maximum(s - n_f, 0))),
            pl.BlockSpec((1, 1, ff, td), lambda ei, s: (layer, ei, 0, jnp.maximum(s - n_f, 0) + n_d2)),
        ] + [pl.BlockSpec(memory_space=pl.ANY) for _ in srcs],
        out_specs=pl.BlockSpec((1, r, td), lambda ei, s: (ei, 0, jnp.maximum(s - n_f, 0))),
        out_shape=jax.ShapeDtypeStruct((e, r, d // 2), jnp.uint32),
        scratch_shapes=[
            pltpu.VMEM((r, d // 2), jnp.uint32),
            pltpu.VMEM((r, d), BF16),
            pltpu.VMEM((n_f, r, tf), BF16),
            pltpu.SemaphoreType.DMA(()),
        ],
        compiler_params=_cparams(("arbitrary", "arbitrary")),
        name="expert_ffn",
    )(idx_all, idx_all, w_gate, w_up, w_down, w_down, *srcs)


def _select_kernel(aff_ref, w_ref, idx_ref, bnd_ref, pos_ref, post_ref, *, cap, tile, jchunk):
    n = aff_ref.shape[1]
    bits = pltpu.bitcast(aff_ref[0], jnp.int32)

    def search(it, thr):
        cand = thr | jnp.left_shift(jnp.int32(1), 30 - it)
        cnt = jnp.sum(jnp.where(bits >= cand, 1, 0), axis=0, keepdims=True)
        return jnp.where(cnt >= cap, cand, thr)

    thr = lax.fori_loop(0, 31, search, jnp.zeros((1, LANES), jnp.int32))
    need = (cap - jnp.sum(jnp.where(bits > thr, 1, 0), axis=0, keepdims=True)).astype(F32)

    r = lax.broadcasted_iota(jnp.int32, (tile, tile), 0)
    c = lax.broadcasted_iota(jnp.int32, (tile, tile), 1)
    tri = jnp.where(c <= r, 1.0, 0.0).astype(BF16)

    def chunk(k, carry):
        ceq, cpos = carry
        r0 = pl.multiple_of(k * tile, tile)
        a = aff_ref[0, pl.ds(r0, tile), :]
        bt = pltpu.bitcast(a, jnp.int32)
        eq = bt == thr
        rank = _dot(tri, jnp.where(eq, 1.0, 0.0).astype(BF16)) + ceq
        sel = jnp.where((bt > thr) | (eq & (rank <= need)), 1.0, 0.0)
        pos = _dot(tri, sel.astype(BF16)) + cpos
        pos_ref[pl.ds(r0, tile), :] = pos
        w_ref[0, pl.ds(r0, tile), :] = sel * a
        cpos = pos[tile - 1:tile, :]
        bnd_ref[0, pl.ds(k, 1), :] = cpos.astype(jnp.int32)
        return rank[tile - 1:tile, :], cpos

    zero = jnp.zeros((1, LANES), F32)
    lax.fori_loop(0, n // tile, chunk, (zero, zero))

    for k in range(n // tile):
        post_ref[:, k * tile:(k + 1) * tile] = pos_ref[k * tile:(k + 1) * tile, :].T

    def per_expert(e, carry):
        row = post_ref[pl.ds(e, 1), :]

        def per_j(jc, carry_j):
            j0 = pl.multiple_of(jc * jchunk, jchunk)
            jv = (lax.broadcasted_iota(jnp.int32, (jchunk, n), 0) + j0).astype(F32)
            cnt = jnp.sum(jnp.where(row <= jv, 1.0, 0.0), axis=-1, keepdims=True)
            idx_ref[0, e, pl.ds(j0, jchunk), :] = cnt.astype(jnp.int32)
            return carry_j

        return lax.fori_loop(0, cap // jchunk, per_j, carry)

    lax.fori_loop(0, N_EXPERTS, per_expert, 0)


def _select(aff, cap, tile):
    b, n, _ = aff.shape
    jchunk = min(64, cap)
    kern = functools.partial(_select_kernel, cap=cap, tile=tile, jchunk=jchunk)
    return pl.pallas_call(
        kern,
        grid=(b,),
        in_specs=[pl.BlockSpec((1, n, LANES), lambda bi: (bi, 0, 0))],
        out_specs=[pl.BlockSpec((1, n, LANES), lambda bi: (bi, 0, 0)),
                   pl.BlockSpec((1, N_EXPERTS, cap, 1), lambda bi: (bi, 0, 0, 0)),
                   pl.BlockSpec((1, n // tile, LANES), lambda bi: (bi, 0, 0))],
        out_shape=[jax.ShapeDtypeStruct((b, n, LANES), F32),
                   jax.ShapeDtypeStruct((b, N_EXPERTS, cap, 1), jnp.int32),
                   jax.ShapeDtypeStruct((b, n // tile, LANES), jnp.int32)],
        scratch_shapes=[pltpu.VMEM((n, LANES), F32), pltpu.VMEM((LANES, n), F32)],
        compiler_params=_cparams(("arbitrary",)),
        name="select",
    )(aff)


def _combine_kernel(idx_ref, bnd_ref, x_ref, g2_ref, w_ref, *rest, tile, cap, n_tiles, row0, rows_per_expert,
                    final_norm):
    if final_norm:
        fg_ref, y_hbm, o_ref, buf, cnt_ref, sem = rest
    else:
        y_hbm, o_ref, buf, cnt_ref, sem = rest
    b = pl.program_id(0)
    i = pl.program_id(1)
    plane_rows = N_EXPERTS * tile
    unroll = DMA_ISSUE_UNROLL

    @pl.when((b == 0) & (i == 0))
    def _init():
        buf[...] = jnp.zeros(buf.shape, jnp.uint32)

    @pl.when(i < n_tiles)
    def _issue():
        slot = i % 2
        dst = buf.at[slot]
        dsem = sem.at[slot]
        t0 = i * tile
        total = jnp.int32(0)
        for e in range(N_EXPERTS):
            lo = bnd_ref[0, i, e]
            hi = bnd_ref[0, i + 1, e]
            src0 = e * rows_per_expert + row0 + b * cap
            groups = lax.shift_right_logical(hi - lo + (unroll - 1), unroll.bit_length() - 1)

            def group(k, carry, e=e, lo=lo, hi=hi, src0=src0):
                for u in range(unroll):
                    j = lo + k * unroll + u
                    ok = j < hi
                    jj = jnp.minimum(j, cap - 1)
                    drow = jnp.where(ok, e * tile + idx_ref[0, e, jj] - t0, plane_rows + unroll * e + u)
                    srow = jnp.where(ok, src0 + jj, 0)
                    pltpu.make_async_copy(y_hbm.at[pl.ds(srow, 1), :], dst.at[pl.ds(drow, 1), :], dsem).start()
                return carry

            lax.fori_loop(0, groups, group, 0)
            total = total + groups * unroll

        pad = (-total) & (SUBLANES - 1)

        def issue_pad(k, carry):
            pltpu.make_async_copy(y_hbm.at[pl.ds(k, 1), :],
                                  dst.at[pl.ds(plane_rows + unroll * N_EXPERTS + k, 1), :], dsem).start()
            return carry

        lax.fori_loop(0, pad, issue_pad, 0)
        cnt_ref[slot] = total + pad

    @pl.when(i > 0)
    def _reduce():
        slot = (i - 1) % 2
        n_rows = pl.multiple_of(cnt_ref[slot], SUBLANES)

        @pl.when(n_rows > 0)
        def _wait():
            pltpu.make_async_copy(y_hbm.at[pl.ds(0, n_rows), :], buf.at[slot, pl.ds(0, n_rows), :],
                                  sem.at[slot]).wait()

        def rows(g, carry):
            r0 = pl.multiple_of(g * COMBINE_ROWS, COMBINE_ROWS)
            wv = w_ref[0, pl.ds(r0, COMBINE_ROWS), :]
            acc_hi = None
            acc_lo = None
            for e in range(N_EXPERTS):
                wcol = wv[:, e:e + 1]
                hi, lo = _unpack_pair(buf[slot, pl.ds(e * tile + r0, COMBINE_ROWS), :])
                acc_hi = hi * wcol if acc_hi is None else acc_hi + hi * wcol
                acc_lo = lo * wcol if acc_lo is None else acc_lo + lo * wcol
            acc = jnp.concatenate([acc_hi, acc_lo], axis=1)
            out = x_ref[0, pl.ds(r0, COMBINE_ROWS), :] + g2_ref[0] * acc
            if final_norm:
                out = _rms(out, fg_ref[...])
            o_ref[0, pl.ds(r0, COMBINE_ROWS), :] = out
            return carry

        lax.fori_loop(0, tile // COMBINE_ROWS, rows, 0)


def _combine(idx, bnd, x, g2, w, y_flat, tile, row0, rows_per_expert, final_g=None):
    b, n, d = x.shape
    cap = idx.shape[-1]
    n_tiles = n // tile
    final_norm = final_g is not None
    kern = functools.partial(_combine_kernel, tile=tile, cap=cap, n_tiles=n_tiles, row0=row0,
                             rows_per_expert=rows_per_expert, final_norm=final_norm)
    row = lambda bi, i: (bi, jnp.maximum(i - 1, 0), 0)
    per_b = lambda bi, i: (bi, 0, 0)
    spare = DMA_ISSUE_UNROLL * N_EXPERTS + SUBLANES
    in_specs = [
        pl.BlockSpec((1, N_EXPERTS, cap), per_b, memory_space=pltpu.SMEM),
        pl.BlockSpec((1, n // tile + 1, N_EXPERTS), per_b, memory_space=pltpu.SMEM),
        pl.BlockSpec((1, tile, d), row),
        pl.BlockSpec((1, 1, d), per_b),
        pl.BlockSpec((1, tile, LANES), row),
    ]
    args = [idx, bnd, x, g2, w]
    if final_norm:
        in_specs.append(pl.BlockSpec((1, d), lambda bi, i: (0, 0)))
        args.append(final_g)
    in_specs.append(pl.BlockSpec(memory_space=pl.ANY))
    args.append(y_flat)
    return pl.pallas_call(
        kern,
        grid=(b, n_tiles + 1),
        in_specs=in_specs,
        out_specs=pl.BlockSpec((1, tile, d), row),
        out_shape=jax.ShapeDtypeStruct((b, n, d), F32),
        scratch_shapes=[pltpu.VMEM((2, N_EXPERTS * tile + spare, d // 2), jnp.uint32),
                        pltpu.SMEM((2,), jnp.int32), pltpu.SemaphoreType.DMA((2,))],
        compiler_params=_cparams(("arbitrary", "arbitrary")),
        name="combine",
    )(*args)


def _route(aff, cap, tile, row_stride):
    b = aff.shape[0]
    w, idx, bnd = _select(aff, cap, tile)
    idx = idx.reshape(b, N_EXPERTS, cap)
    bnd = jnp.pad(bnd[:, :, :N_EXPERTS], ((0, 0), (1, 0), (0, 0)))
    rows = idx + (jnp.arange(b, dtype=jnp.int32) * row_stride)[:, None, None]
    rows = jnp.swapaxes(rows, 0, 1).reshape(N_EXPERTS, b * cap)
    return w, idx, bnd, rows


def _axial_tables(n_tok, rot_dim):
    rows = (jnp.arange(n_tok, dtype=jnp.int32) // GRID_W).astype(F32)
    cols = (jnp.arange(n_tok, dtype=jnp.int32) % GRID_W).astype(F32)
    n_freq = rot_dim // 4
    inv_freq = ROPE_THETA ** (-jnp.arange(n_freq, dtype=F32) / n_freq)
    ang = jnp.concatenate([rows[:, None] * inv_freq, cols[:, None] * inv_freq], axis=-1)
    return jnp.cos(ang), jnp.sin(ang)


def _rope_tables(n_tok, identity):
    if identity:
        one = jnp.ones((n_tok, LANES), F32)
        zero = jnp.zeros((n_tok, LANES), F32)
        return one, zero, one, zero
    cos_s, sin_s = _axial_tables(n_tok, SWA_HEAD_DIM)
    css = jnp.tile(jnp.concatenate([cos_s, cos_s], axis=-1), (1, 2))
    sns = jnp.tile(jnp.concatenate([-sin_s, sin_s], axis=-1), (1, 2))
    cos_m, sin_m = _axial_tables(n_tok, MLA_ROPE)
    ones = jnp.ones((n_tok, MLA_NOPE), F32)
    tail = LANES - MLA_NOPE - MLA_ROPE
    csm = jnp.concatenate([ones, cos_m, cos_m, jnp.ones((n_tok, tail), F32)], axis=-1)
    snm = jnp.concatenate([0.0 * ones, -sin_m, sin_m, jnp.zeros((n_tok, tail), F32)], axis=-1)
    return css, sns, csm, snm


def _mla_weight_slots(w_uq, w_ukv):
    dq = MLA_NOPE + MLA_ROPE
    wuq = jnp.pad(w_uq.reshape(MLA_Q_LORA, MLA_HEADS, dq), ((0, 0), (0, 0), (0, LANES - dq)))
    wkv = w_ukv.reshape(MLA_KV_LORA, MLA_HEADS, MLA_NOPE + MLA_V)
    wuk = jnp.pad(wkv[..., :MLA_NOPE], ((0, 0), (0, 0), (0, LANES - MLA_NOPE)))
    wv = wkv[..., MLA_NOPE:]
    even = jnp.pad(wv, ((0, 0), (0, 0), (0, LANES - MLA_V)))
    odd = jnp.pad(wv, ((0, 0), (0, 0), (LANES - MLA_V, 0)))
    is_even = (jnp.arange(MLA_HEADS) % 2 == 0)[None, :, None]
    wuv = jnp.where(is_even, even, odd)
    flat = lambda w: w.reshape(w.shape[0], MLA_HEADS * LANES).astype(BF16)
    return flat(wuq), flat(wuk), flat(wuv)


def _phase_tables(k, m, period):
    ang = ((k[:, None] * m[None, :]) % period).astype(F32) * (2.0 * math.pi / period)
    return jnp.cos(ang), jnp.sin(ang)


def _dft_tables(n_tok):
    k = jnp.arange(n_tok, dtype=jnp.int32)
    side = math.isqrt(n_tok)
    if side * side != n_tok:
        c, s = _phase_tables(k, k, n_tok)
        return c.astype(BF16), s.astype(BF16)
    m = jnp.arange(side, dtype=jnp.int32)
    ca, sa = _phase_tables(k, m * side, n_tok)
    cb, sb = _phase_tables(k, m, n_tok)
    c = ca[:, :, None] * cb[:, None, :] - sa[:, :, None] * sb[:, None, :]
    s = sa[:, :, None] * cb[:, None, :] + ca[:, :, None] * sb[:, None, :]
    return c.reshape(n_tok, n_tok).astype(BF16), s.reshape(n_tok, n_tok).astype(BF16)


def _channel_dft(n_tok):
    ch = jnp.arange(GROUP_W, dtype=jnp.int32)
    c, s = _phase_tables(ch % FNET_HEAD_DIM, ch % FNET_HEAD_DIM, FNET_HEAD_DIM)
    same = (ch[:, None] // FNET_HEAD_DIM) == (ch[None, :] // FNET_HEAD_DIM)
    scale = (n_tok * FNET_HEAD_DIM) ** -0.5
    return (jnp.concatenate([jnp.where(same, c, 0.0), jnp.where(same, -s, 0.0)], axis=1) * scale).astype(BF16)


def kernel(x, c, ctx, c_ctx, ada_w, ada_b, norm1_g, norm2_g, w_in, conv_w, swa_sink, mla_q_norm_g, mla_w_uq,
           mla_kv_norm_g, mla_w_ukv, out_norm_g, w_out, w_router, w_gate, w_up, w_down, final_norm_g):
    b, n, d = x.shape
    l = ctx.shape[1]
    depth = ada_w.shape[0]
    cap_x = EC_CAPACITY * n // N_EXPERTS
    cap_c = EC_CAPACITY * l // N_EXPERTS
    tile_x = min(256, n)
    tile_c = min(256, l)

    c_rows = jnp.concatenate([c, c_ctx[None, :], jnp.zeros((-(b + 1) % 8, d), F32)], axis=0)
    mods = _adaln(c_rows, ada_w, ada_b)
    rope_x = _rope_tables(n, identity=False)
    rope_c = _rope_tables(l, identity=True)
    dft_x, dft_c = _dft_tables(n), _dft_tables(l)
    cdft_x, cdft_c = _channel_dft(n), _channel_dft(l)

    for layer in range(depth):
        last = layer == depth - 1
        mx = mods[layer, :b].reshape(b, 1, 6, d)
        mc = jnp.broadcast_to(mods[layer, b].reshape(1, 1, 6, d), (b, 1, 6, d))
        sh1, sc1, g1, sh2, sc2, g2 = [mx[:, :, k] for k in range(6)]
        csh1, csc1, cg1, csh2, csc2, cg2 = [mc[:, :, k] for k in range(6)]

        w_in_p = jnp.pad(w_in[layer].astype(BF16), ((0, 0), (0, IN_COLS_PAD - IN_COLS)))
        wuq, wuk, wuv = _mla_weight_slots(mla_w_uq[layer], mla_w_ukv[layer])
        n1 = norm1_g[layer][None, :]
        qg = mla_q_norm_g[layer][None, :]
        kvg = mla_kv_norm_g[layer][None, :]
        px = _mixer_inputs(x, sh1, sc1, n1, w_in_p, qg, kvg, wuq, wuk, wuv, rope_x, cdft_x)
        pc = _mixer_inputs(ctx, csh1, csc1, n1, w_in_p, qg, kvg, wuq, wuk, wuv, rope_c, cdft_c)
        cb, u, sq, skp, svp, fu, mq, mk, mvp = px
        ccb, cu, csq, cskp, csvp, cfu, cmq, cmk, cmvp = pc

        sink = swa_sink[layer]
        og = out_norm_g[layer][None, :]
        wo = w_out[layer].astype(BF16)
        n2 = norm2_g[layer][None, :]
        wr_hi = w_router[layer].astype(BF16)
        wr_lo = (w_router[layer] - wr_hi.astype(F32)).astype(BF16)
        wr = jnp.pad(jnp.concatenate([wr_hi, wr_lo], axis=1), ((0, 0), (0, LANES - 2 * N_EXPERTS)))

        yx = (_swa(sink, sq, cskp, csvp, skp, svp), _fourier(fu, *dft_x), _mla(mq, cmk, cmvp, mk, mvp))
        x, fx, aff = _mixer_out(x, g1, cb, u, conv_w[layer], *yx, og, wo, n2, sh2, sc2, wr)
        srcs = [fx.reshape(b * n, d // 2)]
        seg_rows = [b * cap_x]
        wx, idx_x, bnd_x, rows_x = _route(aff, cap_x, tile_x, n)
        row_parts = [rows_x]
        if not last:
            yc = (_swa(sink, csq, cskp, csvp), _fourier(cfu, *dft_c), _mla(cmq, cmk, cmvp))
            ctx, fc, affc = _mixer_out(ctx, cg1, ccb, cu, conv_w[layer], *yc, og, wo, n2, csh2, csc2, wr)
            srcs.append(fc.reshape(b * l, d // 2))
            seg_rows.append(b * cap_c)
            wc, idx_c, bnd_c, rows_c = _route(affc, cap_c, tile_c, l)
            row_parts.append(rows_c)
        idx_all = jnp.concatenate(row_parts, axis=1)[:, None, :]
        rows_per_expert = idx_all.shape[-1]
        y_flat = _expert_ffn(idx_all, layer, w_gate, w_up, w_down, srcs, seg_rows).reshape(
            N_EXPERTS * rows_per_expert, d // 2)
        x = _combine(idx_x, bnd_x, x, g2, wx, y_flat, tile_x, 0, rows_per_expert,
                     final_g=final_norm_g[None, :] if last else None)
        if not last:
            ctx = _combine(idx_c, bnd_c, ctx, cg2, wc, y_flat, tile_c, b * cap_x, rows_per_expert)
    return x
```

```python
import functools
import math

import jax
import jax.numpy as jnp
from jax import lax
from jax.experimental import pallas as pl
from jax.experimental.pallas import tpu as pltpu

F32 = jnp.float32
BF16 = jnp.bfloat16
HIGHEST = lax.Precision.HIGHEST

GRID_W = 64
GROUP_W = 512
N_GROUPS = 4
SWA_HEAD_DIM = 64
SWA_HEADS = 8
SWA_KV_HEADS = 2
SWA_BLOCK = 128
FNET_HEAD_DIM = 64
MLA_HEADS = 8
MLA_NOPE = 64
MLA_ROPE = 32
MLA_V = 64
MLA_Q_LORA = 512
MLA_KV_LORA = 256
N_EXPERTS = 16
EC_CAPACITY = 2
ROPE_THETA = 10000.0
EPS = 1e-6
IN_COLS = 3616
IN_COLS_PAD = 3712
LANES = 128
SUBLANES = 8
DMA_ISSUE_UNROLL = 4
COMBINE_ROWS = 16
FFN_UNPACK_ROWS = 512
CONV_HALO = 16
NEG = -1e30
LOG2E = 1.4426950408889634

VMEM_LIMIT = 60 * 1024 * 1024


def _cparams(sem):
    return pltpu.CompilerParams(dimension_semantics=sem, vmem_limit_bytes=VMEM_LIMIT)


def _dot(a, b):
    return jnp.dot(a, b, preferred_element_type=F32)


def _dot_t(a, b):
    return lax.dot_general(a, b, (((1,), (1,)), ((), ())), preferred_element_type=F32)


def _rms(x, g):
    return x * lax.rsqrt(jnp.mean(x * x, axis=-1, keepdims=True) + EPS) * g


def _pack_pair(hi, lo):
    return pltpu.pack_elementwise([hi, lo], packed_dtype=BF16)


def _unpack_pair(v):
    return (pltpu.unpack_elementwise(v, index=0, packed_dtype=BF16, unpacked_dtype=F32),
            pltpu.unpack_elementwise(v, index=1, packed_dtype=BF16, unpacked_dtype=F32))


def _slab_rows(d):
    return d // (2 * LANES)


def _adaln_kernel(c_ref, w_ref, b_ref, o_ref):
    c = c_ref[...]
    s = c / (1.0 + jnp.exp(-c))
    o_ref[0] = lax.dot_general(s, w_ref[0], (((1,), (0,)), ((), ())),
                               precision=HIGHEST, preferred_element_type=F32) + b_ref[0]


def _adaln(c_rows, ada_w, ada_b):
    depth, d, d6 = ada_w.shape
    tn = 1024 if d6 % 1024 == 0 else d6
    rows = c_rows.shape[0]
    return pl.pallas_call(
        _adaln_kernel,
        grid=(depth, d6 // tn),
        in_specs=[
            pl.BlockSpec((rows, d), lambda l, j: (0, 0)),
            pl.BlockSpec((1, d, tn), lambda l, j: (l, 0, j)),
            pl.BlockSpec((1, 1, tn), lambda l, j: (l, 0, j)),
        ],
        out_specs=pl.BlockSpec((1, rows, tn), lambda l, j: (l, 0, j)),
        out_shape=jax.ShapeDtypeStruct((depth, rows, d6), F32),
        compiler_params=_cparams(("arbitrary", "arbitrary")),
        name="adaln",
    )(c_rows, ada_w, ada_b.reshape(depth, 1, d6))


def _swap_halves(x, width):
    lane = lax.broadcasted_iota(jnp.int32, x.shape, 1)
    first = (lane % (2 * width)) < width
    return jnp.where(first, pltpu.roll(x, LANES - width, 1), pltpu.roll(x, width, 1))


def _mixin_kernel(x_ref, sh_ref, sc_ref, g_ref, w_ref, qg_ref, kvg_ref, wuq_ref, wuk_ref, wuv_ref,
                  css_ref, sns_ref, csm_ref, snm_ref, cdft_ref,
                  cb_ref, u_ref, sq_ref, skp_ref, svp_ref, fu_ref, mq_ref, mk_ref, mvp_ref):
    x = x_ref[0]
    h = _rms(x, g_ref[...]) * (1.0 + sc_ref[0]) + sh_ref[0]
    hb = h.astype(BF16)

    pc = _dot(hb, w_ref[:, 0:3 * GROUP_W])
    cb_ref[0] = pc[:, 0:GROUP_W].astype(BF16)
    u_ref[0] = (pc[:, GROUP_W:2 * GROUP_W] * pc[:, 2 * GROUP_W:3 * GROUP_W]).astype(BF16)

    c0 = 3 * GROUP_W
    ps = _dot(hb, w_ref[:, c0:c0 + 768])
    css = css_ref[...]
    sns = sns_ref[...]
    qscale = SWA_HEAD_DIM ** -0.5 * LOG2E
    for c in range(4):
        t = ps[:, c * LANES:(c + 1) * LANES]
        t = (t * css + _swap_halves(t, SWA_HEAD_DIM // 2) * sns) * qscale
        sq_ref[0, :, c * LANES:(c + 1) * LANES] = t.astype(BF16)
    sk = ps[:, 512:640]
    sk = sk * css + _swap_halves(sk, SWA_HEAD_DIM // 2) * sns
    sv = ps[:, 640:768]
    lane = lax.broadcasted_iota(jnp.int32, sk.shape, 1)
    lo = lane < SWA_HEAD_DIM
    for src, dst, fill in ((sk, skp_ref, 0.0), (sv, svp_ref, 1.0)):
        sw = pltpu.roll(src, SWA_HEAD_DIM, 1)
        pad_hi = jnp.where(lane == SWA_HEAD_DIM, fill, 0.0)
        pad_lo = jnp.where(lane == 0, fill, 0.0)
        dst[0, :, 0:128] = jnp.where(lo, src, pad_hi).astype(BF16)
        dst[0, :, 128:256] = jnp.where(lo, pad_lo, sw).astype(BF16)
        dst[0, :, 256:384] = jnp.where(lo, sw, pad_hi).astype(BF16)
        dst[0, :, 384:512] = jnp.where(lo, pad_lo, src).astype(BF16)

    c0 += 768
    fu = _dot(hb, w_ref[:, c0:c0 + GROUP_W]).astype(BF16)
    fu_ref[0] = _dot(fu, cdft_ref[...]).astype(BF16)

    c0 += GROUP_W
    pm = _dot(hb, w_ref[:, c0:c0 + 896])
    qn = _rms(pm[:, 0:MLA_Q_LORA], qg_ref[...]).astype(BF16)
    kvn = _rms(pm[:, MLA_Q_LORA:MLA_Q_LORA + MLA_KV_LORA], kvg_ref[...]).astype(BF16)
    csm = csm_ref[...]
    snm = snm_ref[...]

    def rope_m(t):
        lane_m = lax.broadcasted_iota(jnp.int32, t.shape, 1)
        sw = jnp.where(lane_m < MLA_NOPE + MLA_ROPE // 2,
                       pltpu.roll(t, LANES - MLA_ROPE // 2, 1), pltpu.roll(t, MLA_ROPE // 2, 1))
        return t * csm + sw * snm

    q = _dot(qn, wuq_ref[...])
    mscale = (MLA_NOPE + MLA_ROPE) ** -0.5 * LOG2E
    kpe = rope_m(pltpu.roll(pm[:, 768:896], MLA_NOPE, 1))
    kn = _dot(kvn, wuk_ref[...])
    mv = _dot(kvn, wuv_ref[...])
    lane_v = lax.broadcasted_iota(jnp.int32, kpe.shape, 1)
    for hd in range(MLA_HEADS):
        sl = slice(hd * LANES, (hd + 1) * LANES)
        mq_ref[0, :, sl] = (rope_m(q[:, sl]) * mscale).astype(BF16)
        mk_ref[0, :, sl] = (kn[:, sl] + kpe).astype(BF16)
        ones_lane = MLA_V if hd % 2 == 0 else 0
        mvp_ref[0, :, sl] = jnp.where(lane_v == ones_lane, 1.0, mv[:, sl]).astype(BF16)


def _mixer_inputs(x, shift, scale, norm_g, w_in_p, qg, kvg, wuq, wuk, wuv, rope_tabs, cdft):
    b, n, d = x.shape
    tm = min(256, n)
    css, sns, csm, snm = rope_tabs
    row = lambda bi, i: (bi, i, 0)
    const2 = lambda bi, i: (0, 0)
    tab = lambda bi, i: (i, 0)
    outs = [jax.ShapeDtypeStruct((b, n, w), BF16) for w in (512, 512, 512, 512, 512, 1024, 1024, 1024, 1024)]
    return pl.pallas_call(
        _mixin_kernel,
        grid=(b, n // tm),
        in_specs=[
            pl.BlockSpec((1, tm, d), row),
            pl.BlockSpec((1, 1, d), lambda bi, i: (bi, 0, 0)),
            pl.BlockSpec((1, 1, d), lambda bi, i: (bi, 0, 0)),
            pl.BlockSpec((1, d), const2),
            pl.BlockSpec((d, IN_COLS_PAD), const2),
            pl.BlockSpec((1, MLA_Q_LORA), const2),
            pl.BlockSpec((1, MLA_KV_LORA), const2),
            pl.BlockSpec((MLA_Q_LORA, 1024), const2),
            pl.BlockSpec((MLA_KV_LORA, 1024), const2),
            pl.BlockSpec((MLA_KV_LORA, 1024), const2),
            pl.BlockSpec((tm, LANES), tab),
            pl.BlockSpec((tm, LANES), tab),
            pl.BlockSpec((tm, LANES), tab),
            pl.BlockSpec((tm, LANES), tab),
            pl.BlockSpec((GROUP_W, 2 * GROUP_W), const2),
        ],
        out_specs=[pl.BlockSpec((1, tm, o.shape[-1]), row) for o in outs],
        out_shape=outs,
        compiler_params=_cparams(("arbitrary", "arbitrary")),
        name="mixer_inputs",
    )(x, shift, scale, norm_g, w_in_p, qg, kvg, wuq, wuk, wuv, css, sns, csm, snm, cdft)


def _fourier_kernel(ct_ref, st_ref, ab_ref, o_ref):
    acc = _dot(ct_ref[...], ab_ref[0, :, 0:GROUP_W]) + _dot(st_ref[...], ab_ref[0, :, GROUP_W:2 * GROUP_W])
    o_ref[0] = acc.astype(BF16)


def _fourier(ab, ctab, stab):
    b, n, _ = ab.shape
    tk = min(512, n)
    return pl.pallas_call(
        _fourier_kernel,
        grid=(b, n // tk),
        in_specs=[
            pl.BlockSpec((tk, n), lambda bi, i: (i, 0)),
            pl.BlockSpec((tk, n), lambda bi, i: (i, 0)),
            pl.BlockSpec((1, n, 2 * GROUP_W), lambda bi, i: (bi, 0, 0)),
        ],
        out_specs=pl.BlockSpec((1, tk, GROUP_W), lambda bi, i: (bi, i, 0)),
        out_shape=jax.ShapeDtypeStruct((b, n, GROUP_W), BF16),
        compiler_params=_cparams(("arbitrary", "arbitrary")),
        name="fourier",
    )(ctab, stab, ab)


def _swa_kernel(sink_ref, q_ref, kc_ref, vc_ref, *rest, has_band, n_blocks):
    if has_band:
        kp_ref, vp_ref, o_ref = rest
    else:
        (o_ref,) = rest
    i = pl.program_id(1)
    t = SWA_BLOCK
    if has_band:
        start = pl.multiple_of(jnp.clip(i - 1, 0, n_blocks - 3) * t, t)
        qpos = i * t + lax.broadcasted_iota(jnp.int32, (2 * t, 3 * t), 0) % t
        kpos = start + lax.broadcasted_iota(jnp.int32, (2 * t, 3 * t), 1)
        valid = jnp.abs(kpos - qpos) <= SWA_BLOCK
    first = lax.broadcasted_iota(jnp.int32, (2 * t, 1), 0) < t
    lane = lax.broadcasted_iota(jnp.int32, (2 * t, LANES), 1)
    acc = [None] * 4
    for g in range(SWA_KV_HEADS):
        for hh in range(2):
            slot = 2 * g + hh
            sl = slice(slot * LANES, (slot + 1) * LANES)
            cols = (2 * g, 2 * g + 1)
            q2 = jnp.concatenate([q_ref[0, :, c * LANES:(c + 1) * LANES] for c in cols], axis=0)
            sink = jnp.where(first, sink_ref[2 * cols[0] + hh], sink_ref[2 * cols[1] + hh]) * LOG2E
            s_ctx = _dot_t(q2, kc_ref[0, :, sl])
            m = jnp.maximum(jnp.max(s_ctx, axis=-1, keepdims=True), sink)
            if has_band:
                s_b = jnp.where(valid, _dot_t(q2, kp_ref[0, pl.ds(start, 3 * t), sl]), NEG)
                m = jnp.maximum(m, jnp.max(s_b, axis=-1, keepdims=True))
            o = _dot(jnp.exp2(s_ctx - m).astype(BF16), vc_ref[0, :, sl])
            if has_band:
                o = o + _dot(jnp.exp2(s_b - m).astype(BF16), vp_ref[0, pl.ds(start, 3 * t), sl])
            ones_lane = SWA_HEAD_DIM if hh == 0 else 0
            den = o[:, ones_lane:ones_lane + 1] + jnp.exp2(sink - m)
            keep = (lane < SWA_HEAD_DIM) if hh == 0 else (lane >= SWA_HEAD_DIM)
            o = jnp.where(keep, o / den, 0.0)
            for k, c in enumerate(cols):
                part = o[k * t:(k + 1) * t]
                acc[c] = part if acc[c] is None else acc[c] + part
    for c in range(4):
        o_ref[0, :, c * LANES:(c + 1) * LANES] = acc[c].astype(BF16)


def _swa(sink, q, kc, vc, kp=None, vp=None):
    b, n, _ = q.shape
    l = kc.shape[1]
    t = SWA_BLOCK
    has_band = kp is not None
    blk = lambda bi, i: (bi, i, 0)
    whole = lambda bi, i: (bi, 0, 0)
    in_specs = [
        pl.BlockSpec(memory_space=pltpu.SMEM),
        pl.BlockSpec((1, t, 512), blk),
        pl.BlockSpec((1, l, 512), whole),
        pl.BlockSpec((1, l, 512), whole),
    ]
    args = [sink, q, kc, vc]
    if has_band:
        in_specs += [pl.BlockSpec((1, n, 512), whole), pl.BlockSpec((1, n, 512), whole)]
        args += [kp, vp]
    return pl.pallas_call(
        functools.partial(_swa_kernel, has_band=has_band, n_blocks=n // t),
        grid=(b, n // t),
        in_specs=in_specs,
        out_specs=pl.BlockSpec((1, t, 512), blk),
        out_shape=jax.ShapeDtypeStruct((b, n, 512), BF16),
        compiler_params=_cparams(("arbitrary", "arbitrary")),
        name="swa_latent" if has_band else "swa_context",
    )(*args)


def _mla_kernel(q_ref, kc_ref, vc_ref, *rest, has_lat):
    if has_lat:
        k_ref, v_ref, o_ref = rest
    else:
        (o_ref,) = rest
    lane = lax.broadcasted_iota(jnp.int32, (q_ref.shape[1], LANES), 1)
    out = None
    for hh in range(2):
        sl = slice(hh * LANES, (hh + 1) * LANES)
        q = q_ref[0, :, sl]
        s_ctx = _dot_t(q, kc_ref[0, :, sl])
        m = jnp.max(s_ctx, axis=-1, keepdims=True)
        if has_lat:
            s_lat = _dot_t(q, k_ref[0, :, sl])
            m = jnp.maximum(m, jnp.max(s_lat, axis=-1, keepdims=True))
        o = _dot(jnp.exp2(s_ctx - m).astype(BF16), vc_ref[0, :, sl])
        if has_lat:
            o = o + _dot(jnp.exp2(s_lat - m).astype(BF16), v_ref[0, :, sl])
        ones_lane = MLA_V if hh == 0 else 0
        keep = (lane < MLA_V) if hh == 0 else (lane >= MLA_V)
        o = jnp.where(keep, o / o[:, ones_lane:ones_lane + 1], 0.0)
        out = o if out is None else out + o
    o_ref[0] = out.astype(BF16)


def _mla(q, kc, vc, k=None, v=None):
    b, n, _ = q.shape
    l = kc.shape[1]
    tq = min(256, n)
    has_lat = k is not None
    pair = 2 * LANES
    in_specs = [
        pl.BlockSpec((1, tq, pair), lambda bi, h, i: (bi, i, h)),
        pl.BlockSpec((1, l, pair), lambda bi, h, i: (bi, 0, h)),
        pl.BlockSpec((1, l, pair), lambda bi, h, i: (bi, 0, h)),
    ]
    args = [q, kc, vc]
    if has_lat:
        in_specs += [pl.BlockSpec((1, n, pair), lambda bi, h, i: (bi, 0, h)),
                     pl.BlockSpec((1, n, pair), lambda bi, h, i: (bi, 0, h))]
        args += [k, v]
    return pl.pallas_call(
        functools.partial(_mla_kernel, has_lat=has_lat),
        grid=(b, MLA_HEADS // 2, n // tq),
        in_specs=in_specs,
        out_specs=pl.BlockSpec((1, tq, LANES), lambda bi, h, i: (bi, i, h)),
        out_shape=jax.ShapeDtypeStruct((b, n, MLA_HEADS // 2 * LANES), BF16),
        compiler_params=_cparams(("arbitrary", "arbitrary", "arbitrary")),
        name="mla_latent" if has_lat else "mla_context",
    )(*args)


def _mixout_kernel(x_ref, g1_ref, cb_ref, u_ref, up_ref, un_ref, cw_ref, ys_ref, yf_ref, ym_ref, og_ref, wo_ref,
                   n2_ref, sh2_ref, sc2_ref, wr_ref, xo_ref, fx_ref, aff_ref):
    i = pl.program_id(1)
    u = u_ref[0].astype(F32)
    tm = u.shape[0]
    prev = jnp.where(i > 0, up_ref[0].astype(F32)[CONV_HALO - 1:CONV_HALO, :], 0.0)
    nxt = jnp.where(i < pl.num_programs(1) - 1, un_ref[0].astype(F32)[0:1, :], 0.0)
    row = lax.broadcasted_iota(jnp.int32, u.shape, 0)
    u_before = jnp.where(row == 0, prev, pltpu.roll(u, 1, 0))
    u_after = jnp.where(row == tm - 1, nxt, pltpu.roll(u, tm - 1, 0))
    conv = cb_ref[0].astype(F32) * (cw_ref[0:1, :] * u_before + cw_ref[1:2, :] * u + cw_ref[2:3, :] * u_after)
    groups = (conv, ys_ref[0].astype(F32), yf_ref[0].astype(F32), ym_ref[0].astype(F32))
    acc = None
    for gi, y in enumerate(groups):
        sl = slice(gi * GROUP_W, (gi + 1) * GROUP_W)
        yn = _rms(y, og_ref[:, sl]).astype(BF16)
        part = _dot(yn, wo_ref[sl, :])
        acc = part if acc is None else acc + part
    xn = x_ref[0] + g1_ref[0] * acc
    xo_ref[0] = xn
    fx = _rms(xn, n2_ref[...]) * (1.0 + sc2_ref[0]) + sh2_ref[0]
    half = fx.shape[1] // 2
    fx_ref[0] = _pack_pair(fx[:, :half], fx[:, half:])
    fh = fx.astype(BF16)
    fl = (fx - fh.astype(F32)).astype(BF16)
    parts = _dot(fh, wr_ref[...]) + _dot(fl, wr_ref[...])
    logits = parts + pltpu.roll(parts, LANES - N_EXPERTS, 1)
    lane = lax.broadcasted_iota(jnp.int32, logits.shape, 1)
    logits = jnp.where(lane < N_EXPERTS, logits, NEG)
    e = jnp.exp(logits - jnp.max(logits, axis=-1, keepdims=True))
    aff_ref[0] = e / jnp.sum(e, axis=-1, keepdims=True)


def _mixer_out(x, g1, cb, u, conv_w, ys, yf, ym, out_norm_g, w_out_b, norm2_g, sh2, sc2, w_router_p):
    b, n, d = x.shape
    tm = min(256, n)
    halo_per_tile = tm // CONV_HALO
    n_halo = n // CONV_HALO
    row = lambda bi, i: (bi, i, 0)
    per_b = lambda bi, i: (bi, 0, 0)
    const2 = lambda bi, i: (0, 0)
    return pl.pallas_call(
        _mixout_kernel,
        grid=(b, n // tm),
        in_specs=[
            pl.BlockSpec((1, tm, d), row),
            pl.BlockSpec((1, 1, d), per_b),
            pl.BlockSpec((1, tm, GROUP_W), row),
            pl.BlockSpec((1, tm, GROUP_W), row),
            pl.BlockSpec((1, CONV_HALO, GROUP_W), lambda bi, i: (bi, jnp.maximum(i * halo_per_tile - 1, 0), 0)),
            pl.BlockSpec((1, CONV_HALO, GROUP_W),
                         lambda bi, i: (bi, jnp.minimum((i + 1) * halo_per_tile, n_halo - 1), 0)),
            pl.BlockSpec((3, GROUP_W), const2),
            pl.BlockSpec((1, tm, GROUP_W), row),
            pl.BlockSpec((1, tm, GROUP_W), row),
            pl.BlockSpec((1, tm, GROUP_W), row),
            pl.BlockSpec((1, N_GROUPS * GROUP_W), const2),
            pl.BlockSpec((N_GROUPS * GROUP_W, d), const2),
            pl.BlockSpec((1, d), const2),
            pl.BlockSpec((1, 1, d), per_b),
            pl.BlockSpec((1, 1, d), per_b),
            pl.BlockSpec((d, LANES), const2),
        ],
        out_specs=[pl.BlockSpec((1, tm, d), row), pl.BlockSpec((1, tm, d // 2), row),
                   pl.BlockSpec((1, tm, LANES), row)],
        out_shape=[jax.ShapeDtypeStruct((b, n, d), F32), jax.ShapeDtypeStruct((b, n, d // 2), jnp.uint32),
                   jax.ShapeDtypeStruct((b, n, LANES), F32)],
        compiler_params=_cparams(("arbitrary", "arbitrary")),
        name="mixer_out",
    )(x, g1, cb, u, u, u, conv_w, ys, yf, ym, out_norm_g, w_out_b, norm2_g, sh2, sc2, w_router_p)


def _ffn_kernel(idx_ref, idx_next_ref, wg_ref, wu_ref, wda_ref, wdb_ref, *rest, seg_rows, n_f, tf, td, n_d2):
    n_src = len(seg_rows)
    srcs = rest[:n_src]
    y_hbm = rest[n_src]
    stage_ref, xs_ref, act_ref, ybuf, sem, ysem = rest[n_src + 1:]
    e = pl.program_id(0)
    s = pl.program_id(1)
    r, d = xs_ref.shape
    half = d // 2
    sr = _slab_rows(d)
    y_dst = y_hbm.at[pl.ds(pl.multiple_of(e * (r * sr), SUBLANES), r * sr), :]

    def start_gather(ids_ref):
        base = 0
        for src, rows in zip(srcs, seg_rows):

            def issue(j, carry, base=base, src=src):
                pltpu.make_async_copy(src.at[pl.ds(ids_ref[0, 0, base + j], 1), :],
                                      stage_ref.at[pl.ds(base + j, 1), :], sem).start()
                return carry

            lax.fori_loop(0, rows, issue, 0, unroll=DMA_ISSUE_UNROLL)
            base += rows

    @pl.when(s == 0)
    def _gather():
        @pl.when(e == 0)
        def _first():
            start_gather(idx_ref)

        pltpu.make_async_copy(srcs[0].at[pl.ds(0, r), :], stage_ref, sem).wait()
        for r0 in range(0, r, FFN_UNPACK_ROWS):
            rn = min(FFN_UNPACK_ROWS, r - r0)
            hi, lo = _unpack_pair(stage_ref[r0:r0 + rn, :])
            xs_ref[r0:r0 + rn, 0:half] = hi.astype(BF16)
            xs_ref[r0:r0 + rn, half:d] = lo.astype(BF16)

        @pl.when(e + 1 < pl.num_programs(0))
        def _prefetch():
            start_gather(idx_next_ref)

    @pl.when(s < n_f)
    def _gate_up():
        xs = xs_ref[...]
        a = _dot(xs, wg_ref[0, 0].astype(BF16))
        u = _dot(xs, wu_ref[0, 0].astype(BF16))
        act_ref[s] = (a / (1.0 + jnp.exp(-a)) * u).astype(BF16)

    @pl.when(s >= n_f)
    def _down():
        k = s - n_f

        @pl.when((k == 0) & (e > 0))
        def _drain_previous():
            pltpu.make_async_copy(ybuf, y_dst, ysem).wait()

        def down(w_ref):
            acc = None
            for j in range(n_f):
                part = _dot(act_ref[j], w_ref[0, 0, j * tf:(j + 1) * tf, :].astype(BF16))
                acc = part if acc is None else acc + part
            return acc

        y_hi = down(wda_ref)
        y_lo = down(wdb_ref)
        for c in range(td // LANES):
            packed = _pack_pair(y_hi[:, c * LANES:(c + 1) * LANES], y_lo[:, c * LANES:(c + 1) * LANES])
            ybuf[pl.ds(k * (td // LANES) + c, r, stride=sr), :] = packed

        @pl.when(k == n_d2 - 1)
        def _write_back():
            pltpu.make_async_copy(ybuf, y_dst, ysem).start()

        @pl.when((k == n_d2 - 1) & (e == pl.num_programs(0) - 1))
        def _drain_last():
            pltpu.make_async_copy(ybuf, y_dst, ysem).wait()


def _expert_ffn(idx_all, layer, w_gate, w_up, w_down, srcs, seg_rows):
    _, e, d, ff = w_gate.shape
    r = idx_all.shape[-1]
    sr = _slab_rows(d)
    tf = min(256, ff)
    td = min(256, d // 2)
    n_f = ff // tf
    n_d2 = (d // 2) // td
    kern = functools.partial(_ffn_kernel, seg_rows=tuple(seg_rows), n_f=n_f, tf=tf, td=td, n_d2=n_d2)
    return pl.pallas_call(
        kern,
        grid=(e, n_f + n_d2),
        in_specs=[
            pl.BlockSpec((1, 1, r), lambda ei, s: (ei, 0, 0), memory_space=pltpu.SMEM),
            pl.BlockSpec((1, 1, r), lambda ei, s: (jnp.minimum(ei + 1, e - 1), 0, 0), memory_space=pltpu.SMEM),
            pl.BlockSpec((1, 1, d, tf), lambda ei, s: (layer, ei, 0, jnp.minimum(s, n_f - 1))),
            pl.BlockSpec((1, 1, d, tf), lambda ei, s: (layer, ei, 0, jnp.minimum(s, n_f - 1))),
            pl.BlockSpec((1, 1, ff, td), lambda ei, s: (layer, ei, 0, jnp.maximum(s - n_f, 0))),
            pl.BlockSpec((1, 1, ff, td), lambda ei, s: (layer, ei, 0, jnp.maximum(s - n_f, 0) + n_d2)),
        ] + [pl.BlockSpec(memory_space=pl.ANY) for _ in srcs],
        out_specs=pl.BlockSpec(memory_space=pl.ANY),
        out_shape=jax.ShapeDtypeStruct((e * r * sr, LANES), jnp.uint32),
        scratch_shapes=[
            pltpu.VMEM((r, d // 2), jnp.uint32),
            pltpu.VMEM((r, d), BF16),
            pltpu.VMEM((n_f, r, tf), BF16),
            pltpu.VMEM((r * sr, LANES), jnp.uint32),
            pltpu.SemaphoreType.DMA(()),
            pltpu.SemaphoreType.DMA(()),
        ],
        compiler_params=_cparams(("arbitrary", "arbitrary")),
        name="expert_ffn",
    )(idx_all, idx_all, w_gate, w_up, w_down, w_down, *srcs)


def _select_kernel(aff_ref, w_ref, idx_ref, bnd_ref, pos_ref, post_ref, *, cap, tile, jchunk):
    n = aff_ref.shape[1]
    aff = aff_ref[0]

    def search(it, thr_bits):
        cand = thr_bits | jnp.left_shift(jnp.int32(1), 30 - it)
        cnt = jnp.sum(jnp.where(aff >= pltpu.bitcast(cand, F32), 1, 0), axis=0, keepdims=True)
        return jnp.where(cnt >= cap, cand, thr_bits)

    thr = pltpu.bitcast(lax.fori_loop(0, 31, search, jnp.zeros((1, LANES), jnp.int32)), F32)
    need = (cap - jnp.sum(jnp.where(aff > thr, 1, 0), axis=0, keepdims=True)).astype(F32)

    r = lax.broadcasted_iota(jnp.int32, (tile, tile), 0)
    c = lax.broadcasted_iota(jnp.int32, (tile, tile), 1)
    tri = jnp.where(c <= r, 1.0, 0.0).astype(BF16)

    def chunk(k, carry):
        ceq, cpos = carry
        r0 = pl.multiple_of(k * tile, tile)
        a = aff_ref[0, pl.ds(r0, tile), :]
        eq = a == thr
        rank = _dot(tri, jnp.where(eq, 1.0, 0.0).astype(BF16)) + ceq
        sel = jnp.where((a > thr) | (eq & (rank <= need)), 1.0, 0.0)
        pos = _dot(tri, sel.astype(BF16)) + cpos
        pos_ref[pl.ds(r0, tile), :] = pos
        w_ref[0, pl.ds(r0, tile), :] = sel * a
        cpos = pos[tile - 1:tile, :]
        bnd_ref[0, pl.ds(k, 1), :] = cpos.astype(jnp.int32)
        return rank[tile - 1:tile, :], cpos

    zero = jnp.zeros((1, LANES), F32)
    lax.fori_loop(0, n // tile, chunk, (zero, zero))

    for k in range(n // tile):
        post_ref[:, k * tile:(k + 1) * tile] = pos_ref[k * tile:(k + 1) * tile, :].T

    def per_expert(e, carry):
        row = post_ref[pl.ds(e, 1), :]

        def per_j(jc, carry_j):
            j0 = pl.multiple_of(jc * jchunk, jchunk)
            jv = (lax.broadcasted_iota(jnp.int32, (jchunk, n), 0) + j0).astype(F32)
            cnt = jnp.sum(jnp.where(row <= jv, 1.0, 0.0), axis=-1, keepdims=True)
            idx_ref[0, e, pl.ds(j0, jchunk), :] = cnt.astype(jnp.int32)
            return carry_j

        return lax.fori_loop(0, cap // jchunk, per_j, carry)

    lax.fori_loop(0, N_EXPERTS, per_expert, 0)


def _select(aff, cap, tile):
    b, n, _ = aff.shape
    jchunk = min(64, cap)
    kern = functools.partial(_select_kernel, cap=cap, tile=tile, jchunk=jchunk)
    return pl.pallas_call(
        kern,
        grid=(b,),
        in_specs=[pl.BlockSpec((1, n, LANES), lambda bi: (bi, 0, 0))],
        out_specs=[pl.BlockSpec((1, n, LANES), lambda bi: (bi, 0, 0)),
                   pl.BlockSpec((1, N_EXPERTS, cap, 1), lambda bi: (bi, 0, 0, 0)),
                   pl.BlockSpec((1, n // tile, LANES), lambda bi: (bi, 0, 0))],
        out_shape=[jax.ShapeDtypeStruct((b, n, LANES), F32),
                   jax.ShapeDtypeStruct((b, N_EXPERTS, cap, 1), jnp.int32),
                   jax.ShapeDtypeStruct((b, n // tile, LANES), jnp.int32)],
        scratch_shapes=[pltpu.VMEM((n, LANES), F32), pltpu.VMEM((LANES, n), F32)],
        compiler_params=_cparams(("arbitrary",)),
        name="select",
    )(aff)


def _combine_kernel(idx_ref, bnd_ref, x_ref, g2_ref, w_ref, *rest, tile, cap, n_tiles, row0, rows_per_expert,
                    final_norm):
    if final_norm:
        fg_ref, y_hbm, o_ref, buf, cnt_ref, sem = rest
    else:
        y_hbm, o_ref, buf, cnt_ref, sem = rest
    b = pl.program_id(0)
    i = pl.program_id(1)
    plane_rows = N_EXPERTS * tile
    unroll = DMA_ISSUE_UNROLL
    sr = _slab_rows(x_ref.shape[-1])
    wait_unit = max(1, SUBLANES // sr)

    @pl.when((b == 0) & (i == 0))
    def _init():
        buf[...] = jnp.zeros(buf.shape, jnp.uint32)

    @pl.when(i < n_tiles)
    def _issue():
        slot = i % 2
        dst = buf.at[slot]
        dsem = sem.at[slot]
        t0 = i * tile
        total = jnp.int32(0)
        for e in range(N_EXPERTS):
            lo = bnd_ref[0, i, e]
            hi = bnd_ref[0, i + 1, e]
            src0 = e * rows_per_expert + row0 + b * cap
            groups = lax.shift_right_logical(hi - lo + (unroll - 1), unroll.bit_length() - 1)

            def group(k, carry, e=e, lo=lo, hi=hi, src0=src0):
                for u in range(unroll):
                    j = lo + k * unroll + u
                    ok = j < hi
                    jj = jnp.minimum(j, cap - 1)
                    drow = jnp.where(ok, e * tile + idx_ref[0, e, jj] - t0, plane_rows + unroll * e + u)
                    srow = jnp.where(ok, src0 + jj, 0)
                    pltpu.make_async_copy(y_hbm.at[pl.ds(pl.multiple_of(srow * sr, sr), sr), :],
                                          dst.at[pl.ds(pl.multiple_of(drow * sr, sr), sr), :], dsem).start()
                return carry

            lax.fori_loop(0, groups, group, 0)
            total = total + groups * unroll

        pad = (-total) & (wait_unit - 1)

        def issue_pad(k, carry):
            pltpu.make_async_copy(y_hbm.at[pl.ds(pl.multiple_of(k * sr, sr), sr), :],
                                  dst.at[pl.ds(pl.multiple_of((plane_rows + unroll * N_EXPERTS + k) * sr, sr), sr), :],
                                  dsem).start()
            return carry

        lax.fori_loop(0, pad, issue_pad, 0)
        cnt_ref[slot] = (total + pad) * sr

    @pl.when(i > 0)
    def _reduce():
        slot = (i - 1) % 2
        n_rows = pl.multiple_of(cnt_ref[slot], SUBLANES)

        @pl.when(n_rows > 0)
        def _wait():
            pltpu.make_async_copy(y_hbm.at[pl.ds(0, n_rows), :], buf.at[slot, pl.ds(0, n_rows), :],
                                  sem.at[slot]).wait()

        def rows(g, carry):
            r0 = pl.multiple_of(g * COMBINE_ROWS, COMBINE_ROWS)
            wv = w_ref[0, pl.ds(r0, COMBINE_ROWS), :]
            acc_hi = [None] * sr
            acc_lo = [None] * sr
            for e in range(N_EXPERTS):
                wcol = wv[:, e:e + 1]
                for c in range(sr):
                    hi, lo = _unpack_pair(buf[slot, pl.ds((e * tile + r0) * sr + c, COMBINE_ROWS, stride=sr), :])
                    acc_hi[c] = hi * wcol if acc_hi[c] is None else acc_hi[c] + hi * wcol
                    acc_lo[c] = lo * wcol if acc_lo[c] is None else acc_lo[c] + lo * wcol
            acc = jnp.concatenate(acc_hi + acc_lo, axis=1)
            out = x_ref[0, pl.ds(r0, COMBINE_ROWS), :] + g2_ref[0] * acc
            if final_norm:
                out = _rms(out, fg_ref[...])
            o_ref[0, pl.ds(r0, COMBINE_ROWS), :] = out
            return carry

        lax.fori_loop(0, tile // COMBINE_ROWS, rows, 0)


def _combine(idx, bnd, x, g2, w, y_flat, tile, row0, rows_per_expert, final_g=None):
    b, n, d = x.shape
    cap = idx.shape[-1]
    n_tiles = n // tile
    final_norm = final_g is not None
    kern = functools.partial(_combine_kernel, tile=tile, cap=cap, n_tiles=n_tiles, row0=row0,
                             rows_per_expert=rows_per_expert, final_norm=final_norm)
    row = lambda bi, i: (bi, jnp.maximum(i - 1, 0), 0)
    per_b = lambda bi, i: (bi, 0, 0)
    spare = DMA_ISSUE_UNROLL * N_EXPERTS + SUBLANES
    in_specs = [
        pl.BlockSpec((1, N_EXPERTS, cap), per_b, memory_space=pltpu.SMEM),
        pl.BlockSpec((1, n // tile + 1, N_EXPERTS), per_b, memory_space=pltpu.SMEM),
        pl.BlockSpec((1, tile, d), row),
        pl.BlockSpec((1, 1, d), per_b),
        pl.BlockSpec((1, tile, LANES), row),
    ]
    args = [idx, bnd, x, g2, w]
    if final_norm:
        in_specs.append(pl.BlockSpec((1, d), lambda bi, i: (0, 0)))
        args.append(final_g)
    in_specs.append(pl.BlockSpec(memory_space=pl.ANY))
    args.append(y_flat)
    return pl.pallas_call(
        kern,
        grid=(b, n_tiles + 1),
        in_specs=in_specs,
        out_specs=pl.BlockSpec((1, tile, d), row),
        out_shape=jax.ShapeDtypeStruct((b, n, d), F32),
        scratch_shapes=[pltpu.VMEM((2, (N_EXPERTS * tile + spare) * _slab_rows(d), LANES), jnp.uint32),
                        pltpu.SMEM((2,), jnp.int32), pltpu.SemaphoreType.DMA((2,))],
        compiler_params=_cparams(("arbitrary", "arbitrary")),
        name="combine",
    )(*args)


def _route(aff, cap, tile, row_stride):
    b = aff.shape[0]
    w, idx, bnd = _select(aff, cap, tile)
    idx = idx.reshape(b, N_EXPERTS, cap)
    bnd = jnp.pad(bnd[:, :, :N_EXPERTS], ((0, 0), (1, 0), (0, 0)))
    rows = idx + (jnp.arange(b, dtype=jnp.int32) * row_stride)[:, None, None]
    rows = jnp.swapaxes(rows, 0, 1).reshape(N_EXPERTS, b * cap)
    return w, idx, bnd, rows


def _axial_tables(n_tok, rot_dim):
    rows = (jnp.arange(n_tok, dtype=jnp.int32) // GRID_W).astype(F32)
    cols = (jnp.arange(n_tok, dtype=jnp.int32) % GRID_W).astype(F32)
    n_freq = rot_dim // 4
    inv_freq = ROPE_THETA ** (-jnp.arange(n_freq, dtype=F32) / n_freq)
    ang = jnp.concatenate([rows[:, None] * inv_freq, cols[:, None] * inv_freq], axis=-1)
    return jnp.cos(ang), jnp.sin(ang)


def _rope_tables(n_tok, identity):
    if identity:
        one = jnp.ones((n_tok, LANES), F32)
        zero = jnp.zeros((n_tok, LANES), F32)
        return one, zero, one, zero
    cos_s, sin_s = _axial_tables(n_tok, SWA_HEAD_DIM)
    css = jnp.tile(jnp.concatenate([cos_s, cos_s], axis=-1), (1, 2))
    sns = jnp.tile(jnp.concatenate([-sin_s, sin_s], axis=-1), (1, 2))
    cos_m, sin_m = _axial_tables(n_tok, MLA_ROPE)
    ones = jnp.ones((n_tok, MLA_NOPE), F32)
    tail = LANES - MLA_NOPE - MLA_ROPE
    csm = jnp.concatenate([ones, cos_m, cos_m, jnp.ones((n_tok, tail), F32)], axis=-1)
    snm = jnp.concatenate([0.0 * ones, -sin_m, sin_m, jnp.zeros((n_tok, tail), F32)], axis=-1)
    return css, sns, csm, snm


def _mla_weight_slots(w_uq, w_ukv):
    dq = MLA_NOPE + MLA_ROPE
    wuq = jnp.pad(w_uq.reshape(MLA_Q_LORA, MLA_HEADS, dq), ((0, 0), (0, 0), (0, LANES - dq)))
    wkv = w_ukv.reshape(MLA_KV_LORA, MLA_HEADS, MLA_NOPE + MLA_V)
    wuk = jnp.pad(wkv[..., :MLA_NOPE], ((0, 0), (0, 0), (0, LANES - MLA_NOPE)))
    wv = wkv[..., MLA_NOPE:]
    even = jnp.pad(wv, ((0, 0), (0, 0), (0, LANES - MLA_V)))
    odd = jnp.pad(wv, ((0, 0), (0, 0), (LANES - MLA_V, 0)))
    is_even = (jnp.arange(MLA_HEADS) % 2 == 0)[None, :, None]
    wuv = jnp.where(is_even, even, odd)
    flat = lambda w: w.reshape(w.shape[0], MLA_HEADS * LANES).astype(BF16)
    return flat(wuq), flat(wuk), flat(wuv)


def _phase_tables(k, m, period):
    ang = ((k[:, None] * m[None, :]) % period).astype(F32) * (2.0 * math.pi / period)
    return jnp.cos(ang), jnp.sin(ang)


def _dft_tables(n_tok):
    k = jnp.arange(n_tok, dtype=jnp.int32)
    side = math.isqrt(n_tok)
    if side * side != n_tok:
        c, s = _phase_tables(k, k, n_tok)
        return c.astype(BF16), s.astype(BF16)
    m = jnp.arange(side, dtype=jnp.int32)
    ca, sa = _phase_tables(k, m * side, n_tok)
    cb, sb = _phase_tables(k, m, n_tok)
    c = ca[:, :, None] * cb[:, None, :] - sa[:, :, None] * sb[:, None, :]
    s = sa[:, :, None] * cb[:, None, :] + ca[:, :, None] * sb[:, None, :]
    return c.reshape(n_tok, n_tok).astype(BF16), s.reshape(n_tok, n_tok).astype(BF16)


def _channel_dft(n_tok):
    ch = jnp.arange(GROUP_W, dtype=jnp.int32)
    c, s = _phase_tables(ch % FNET_HEAD_DIM, ch % FNET_HEAD_DIM, FNET_HEAD_DIM)
    same = (ch[:, None] // FNET_HEAD_DIM) == (ch[None, :] // FNET_HEAD_DIM)
    scale = (n_tok * FNET_HEAD_DIM) ** -0.5
    return (jnp.concatenate([jnp.where(same, c, 0.0), jnp.where(same, -s, 0.0)], axis=1) * scale).astype(BF16)


def kernel(x, c, ctx, c_ctx, ada_w, ada_b, norm1_g, norm2_g, w_in, conv_w, swa_sink, mla_q_norm_g, mla_w_uq,
           mla_kv_norm_g, mla_w_ukv, out_norm_g, w_out, w_router, w_gate, w_up, w_down, final_norm_g):
    b, n, d = x.shape
    l = ctx.shape[1]
    depth = ada_w.shape[0]
    cap_x = EC_CAPACITY * n // N_EXPERTS
    cap_c = EC_CAPACITY * l // N_EXPERTS
    tile_x = min(128, n)
    tile_c = min(128, l)

    c_rows = jnp.concatenate([c, c_ctx[None, :], jnp.zeros((-(b + 1) % 8, d), F32)], axis=0)
    mods = _adaln(c_rows, ada_w, ada_b)
    rope_x = _rope_tables(n, identity=False)
    rope_c = _rope_tables(l, identity=True)
    dft_x, dft_c = _dft_tables(n), _dft_tables(l)
    cdft_x, cdft_c = _channel_dft(n), _channel_dft(l)

    for layer in range(depth):
        last = layer == depth - 1
        mx = mods[layer, :b].reshape(b, 1, 6, d)
        mc = jnp.broadcast_to(mods[layer, b].reshape(1, 1, 6, d), (b, 1, 6, d))
        sh1, sc1, g1, sh2, sc2, g2 = [mx[:, :, k] for k in range(6)]
        csh1, csc1, cg1, csh2, csc2, cg2 = [mc[:, :, k] for k in range(6)]

        w_in_p = jnp.pad(w_in[layer].astype(BF16), ((0, 0), (0, IN_COLS_PAD - IN_COLS)))
        wuq, wuk, wuv = _mla_weight_slots(mla_w_uq[layer], mla_w_ukv[layer])
        n1 = norm1_g[layer][None, :]
        qg = mla_q_norm_g[layer][None, :]
        kvg = mla_kv_norm_g[layer][None, :]
        px = _mixer_inputs(x, sh1, sc1, n1, w_in_p, qg, kvg, wuq, wuk, wuv, rope_x, cdft_x)
        pc = _mixer_inputs(ctx, csh1, csc1, n1, w_in_p, qg, kvg, wuq, wuk, wuv, rope_c, cdft_c)
        cb, u, sq, skp, svp, fu, mq, mk, mvp = px
        ccb, cu, csq, cskp, csvp, cfu, cmq, cmk, cmvp = pc

        sink = swa_sink[layer]
        og = out_norm_g[layer][None, :]
        wo = w_out[layer].astype(BF16)
        n2 = norm2_g[layer][None, :]
        wr_hi = w_router[layer].astype(BF16)
        wr_lo = (w_router[layer] - wr_hi.astype(F32)).astype(BF16)
        wr = jnp.pad(jnp.concatenate([wr_hi, wr_lo], axis=1), ((0, 0), (0, LANES - 2 * N_EXPERTS)))

        yx = (_swa(sink, sq, cskp, csvp, skp, svp), _fourier(fu, *dft_x), _mla(mq, cmk, cmvp, mk, mvp))
        x, fx, aff = _mixer_out(x, g1, cb, u, conv_w[layer], *yx, og, wo, n2, sh2, sc2, wr)
        srcs = [fx.reshape(b * n, d // 2)]
        seg_rows = [b * cap_x]
        wx, idx_x, bnd_x, rows_x = _route(aff, cap_x, tile_x, n)
        row_parts = [rows_x]
        if not last:
            yc = (_swa(sink, csq, cskp, csvp), _fourier(cfu, *dft_c), _mla(cmq, cmk, cmvp))
            ctx, fc, affc = _mixer_out(ctx, cg1, ccb, cu, conv_w[layer], *yc, og, wo, n2, csh2, csc2, wr)
            srcs.append(fc.reshape(b * l, d // 2))
            seg_rows.append(b * cap_c)
            wc, idx_c, bnd_c, rows_c = _route(affc, cap_c, tile_c, l)
            row_parts.append(rows_c)
        idx_all = jnp.concatenate(row_parts, axis=1)[:, None, :]
        rows_per_expert = idx_all.shape[-1]
        y_flat = _expert_ffn(idx_all, layer, w_gate, w_up, w_down, srcs, seg_rows)
        x = _combine(idx_x, bnd_x, x, g2, wx, y_flat, tile_x, 0, rows_per_expert,
                     final_g=final_norm_g[None, :] if last else None)
        if not last:
            ctx = _combine(idx_c, bnd_c, ctx, cg2, wc, y_flat, tile_c, b * cap_x, rows_per_expert)
    return x
```
